```python
import math, functools
import jax, jax.numpy as jnp
from jax import lax
import numpy as np

D_MODEL = 2048
BATCH = 2
SEQ = 4096
DEPTH = 4
DEC_BATCH = 8
DEC_SEQ = 4
PAST_LEN = 16384
PAGE_SIZE = 128

W_A = D_MODEL // 4
H_A = 4
BW_A = W_A // H_A
CONV_W = 4
RG_C = 8.0
W_B = D_MODEL // 4
G_B = 4
GW_B = W_B // G_B
CHUNK = 128
W_C = D_MODEL // 2
H_C = 4
HD_C = W_C // H_C // 2
DV_C = 2 * HD_C
Q_BLOCK = 128
W_IN = 2 * W_A + 2 * W_B + 3 * W_C
SPLITS = (W_A, 2 * W_A, 2 * W_A + W_B, 2 * W_A + 2 * W_B, 2 * W_A + 2 * W_B + W_C, 2 * W_A + 2 * W_B + 2 * W_C)
W_OUT_IN = W_A + W_B + W_C
D_FF = -(-8 * D_MODEL // (3 * 256)) * 256
ALPHA = (2.0 * DEPTH) ** 0.25
BETA = (8.0 * DEPTH) ** -0.25
LN_EPS = 1e-5
NEG_INF = -1e30

kernel_name = 'hymba_rglru_gmlp_diffattn_deepnorm_step'


def layer_norm(x, g, b):
    xf = x.astype(jnp.float32)
    mu = jnp.mean(xf, axis=-1, keepdims=True)
    var = jnp.mean(jnp.square(xf - mu), axis=-1, keepdims=True)
    return ((xf - mu) * lax.rsqrt(var + LN_EPS) * g + b).astype(x.dtype)


def rms_norm(x, g):
    xf = x.astype(jnp.float32)
    return (xf * lax.rsqrt(jnp.mean(jnp.square(xf), axis=-1, keepdims=True) + LN_EPS) * g).astype(x.dtype)


def rglru(xa, conv_buf, h0, conv_w, conv_b, w_a, b_a, w_x, b_x, lru_lam):
    bt, t, _ = xa.shape
    xp = jnp.concatenate([conv_buf.astype(xa.dtype), xa], axis=1)
    xc = conv_b + sum(conv_w[j] * xp[:, j:j + t] for j in range(CONV_W))
    xh = xc.reshape(bt, t, H_A, BW_A)
    r = jax.nn.sigmoid(jnp.einsum('bthi,hij->bthj', xh, w_a).reshape(bt, t, W_A) + b_a)
    gi = jax.nn.sigmoid(jnp.einsum('bthi,hij->bthj', xh, w_x).reshape(bt, t, W_A) + b_x)
    log_a = -RG_C * r.astype(jnp.float32) * jax.nn.softplus(-lru_lam.astype(jnp.float32))
    a = jnp.exp(log_a)
    inp = jnp.sqrt(-jnp.expm1(2.0 * log_a)) * (gi * xc).astype(jnp.float32)

    def step(h, ab):
        a_t, b_t = ab
        h = a_t * h + b_t
        return h, h

    h_last, hs = lax.scan(step, h0.astype(jnp.float32), (jnp.moveaxis(a, 1, 0), jnp.moveaxis(inp, 1, 0)))
    return jnp.moveaxis(hs, 0, 1).astype(xa.dtype), h_last.astype(xa.dtype), xp[:, -(CONV_W - 1):]


def chunk_mix(vn, w_s, b_s):
    bt, t, _ = vn.shape
    n_chunks = -(-t // CHUNK)
    pad = n_chunks * CHUNK - t
    vp = jnp.pad(vn, ((0, 0), (0, pad), (0, 0))).reshape(bt, n_chunks, CHUNK, G_B, GW_B)
    causal = jnp.tril(jnp.ones((CHUNK, CHUNK), dtype=bool))
    w = jnp.where(causal[None], w_s, jnp.zeros_like(w_s))
    out = jnp.einsum('gts,bnsgc->bntgc', w, vp) + jnp.transpose(b_s)[None, None, :, :, None]
    return out.reshape(bt, n_chunks * CHUNK, W_B)[:, :t]


def diff_attn_block(q, qpos, k, v, lam):
    s = jnp.einsum('bqhcd,bkhcd->bhcqk', q, k).astype(jnp.float32) * (HD_C ** -0.5)
    kpos = jnp.arange(k.shape[1])
    s = jnp.where(kpos[None, :] <= qpos[:, None], s, NEG_INF)
    p = jax.nn.softmax(s, axis=-1)
    w = p[:, :, 0] - lam * p[:, :, 1]
    return jnp.einsum('bhqk,bkhe->bqhe', w.astype(v.dtype), v)


def diff_attn_prompt(q, k, v, lam):
    bt, t = q.shape[0], q.shape[1]
    nb = t // Q_BLOCK
    qb = jnp.moveaxis(q.reshape(bt, nb, Q_BLOCK, H_C, 2, HD_C), 1, 0)
    o = lax.map(lambda a: diff_attn_block(a[0], a[1] * Q_BLOCK + jnp.arange(Q_BLOCK), k, v, lam),
                (qb, jnp.arange(nb)))
    return jnp.moveaxis(o, 0, 1).reshape(bt, t, H_C, DV_C)


def token_mixer(x, conv_buf, h0, attend, w_in, w_out, conv_w, conv_b, w_a, b_a, w_x, b_x, lru_lam,
                vn_g, vn_b, w_s, b_s, lambda_init, subln_g):
    bt, t, _ = x.shape
    z = jnp.einsum('btd,de->bte', x, w_in)
    xa, ga, ub, vb, q, k, v = jnp.split(z, list(SPLITS), axis=-1)
    lru, h_last, new_buf = rglru(xa, conv_buf, h0, conv_w, conv_b, w_a, b_a, w_x, b_x, lru_lam)
    ya = lru * jax.nn.gelu(ga)
    vn = layer_norm(jax.nn.gelu(vb), vn_g, vn_b)
    yb = jax.nn.gelu(ub) * chunk_mix(vn, w_s, b_s)
    q = q.reshape(bt, t, H_C, 2, HD_C)
    k = k.reshape(bt, t, H_C, 2, HD_C)
    v = v.reshape(bt, t, H_C, DV_C)
    o = rms_norm(attend(q, k, v), subln_g) * (1.0 - lambda_init)
    y = jnp.einsum('bte,ed->btd', jnp.concatenate([ya, yb, o.reshape(bt, t, W_C)], axis=-1), w_out)
    return y, h_last, new_buf, vn, k, v


def swiglu(x, w_gate, w_up, w_down):
    h = jax.nn.silu(jnp.einsum('btd,df->btf', x, w_gate)) * jnp.einsum('btd,df->btf', x, w_up)
    return jnp.einsum('btf,fd->btd', h, w_down)


def setup_inputs(seed: int = 0) -> dict:
    key = jax.random.key(seed)
    ks = jax.random.split(key, 40)
    f32 = jnp.float32
    n_pages = PAST_LEN // PAGE_SIZE
    n_pool = (DEC_BATCH * n_pages * 5) // 4

    def nrm(k, shape, s):
        return jax.random.normal(k, shape, f32) * s

    page_table = jax.random.permutation(ks[0], n_pool)[:DEC_BATCH * n_pages].reshape(DEC_BATCH, n_pages).astype(jnp.int32)
    u = jax.random.uniform(ks[1], (DEPTH, W_A), f32, 0.9, 0.999)
    return {
        'x_prompt': nrm(ks[2], (BATCH, SEQ, D_MODEL), 1.0),
        'x_sample': nrm(ks[3], (DEC_BATCH, DEC_SEQ, D_MODEL), 1.0),
        'cache_k': nrm(ks[4], (DEPTH, n_pool, PAGE_SIZE, H_C, 2, HD_C), 1.0),
        'cache_v': nrm(ks[5], (DEPTH, n_pool, PAGE_SIZE, H_C, DV_C), 1.0),
        'state_lru_h': nrm(ks[6], (DEPTH, DEC_BATCH, W_A), 0.5),
        'state_conv': nrm(ks[7], (DEPTH, DEC_BATCH, CONV_W - 1, W_A), 1.0),
        'page_table': page_table,
        'w_in': nrm(ks[8], (DEPTH, D_MODEL, W_IN), D_MODEL ** -0.5),
        'w_out': nrm(ks[9], (DEPTH, W_OUT_IN, D_MODEL), BETA * W_OUT_IN ** -0.5),
        'conv_w': nrm(ks[10], (DEPTH, CONV_W, W_A), CONV_W ** -0.5),
        'conv_b': nrm(ks[11], (DEPTH, W_A), 0.01),
        'w_a': nrm(ks[12], (DEPTH, H_A, BW_A, BW_A), BW_A ** -0.5),
        'b_a': nrm(ks[13], (DEPTH, W_A), 0.01),
        'w_x': nrm(ks[14], (DEPTH, H_A, BW_A, BW_A), BW_A ** -0.5),
        'b_x': nrm(ks[15], (DEPTH, W_A), 0.01),
        'lru_lam': jnp.log(u) - jnp.log1p(-u),
        'vn_g': 1.0 + nrm(ks[16], (DEPTH, W_B), 0.01),
        'vn_b': nrm(ks[17], (DEPTH, W_B), 0.01),
        'w_s': nrm(ks[18], (DEPTH, G_B, CHUNK, CHUNK), CHUNK ** -0.5),
        'b_s': 1.0 + nrm(ks[19], (DEPTH, G_B, CHUNK), 0.1),
        'lam_q1': nrm(ks[20], (DEPTH, HD_C), 0.1),
        'lam_k1': nrm(ks[21], (DEPTH, HD_C), 0.1),
        'lam_q2': nrm(ks[22], (DEPTH, HD_C), 0.1),
        'lam_k2': nrm(ks[23], (DEPTH, HD_C), 0.1),
        'subln_g': 1.0 + nrm(ks[24], (DEPTH, DV_C), 0.01),
        'ln1_g': 1.0 + nrm(ks[25], (DEPTH, D_MODEL), 0.01),
        'ln1_b': nrm(ks[26], (DEPTH, D_MODEL), 0.01),
        'ln2_g': 1.0 + nrm(ks[27], (DEPTH, D_MODEL), 0.01),
        'ln2_b': nrm(ks[28], (DEPTH, D_MODEL), 0.01),
        'w_gate': nrm(ks[29], (DEPTH, D_MODEL, D_FF), D_MODEL ** -0.5),
        'w_up': nrm(ks[30], (DEPTH, D_MODEL, D_FF), D_MODEL ** -0.5),
        'w_down': nrm(ks[31], (DEPTH, D_FF, D_MODEL), BETA * D_FF ** -0.5),
    }


def reference(x_prompt, x_sample, cache_k, cache_v, state_lru_h, state_conv, page_table,
              w_in, w_out, conv_w, conv_b, w_a, b_a, w_x, b_x, lru_lam, vn_g, vn_b, w_s, b_s,
              lam_q1, lam_k1, lam_q2, lam_k2, subln_g, ln1_g, ln1_b, ln2_g, ln2_b,
              w_gate, w_up, w_down):
    xp, xs = x_prompt, x_sample
    n_prompt, n_dec, t_dec = xp.shape[0], xs.shape[0], xs.shape[1]
    kp_l, vp_l, ks_l, vs_l, hp_l, hs_l, cp_l, cs_l, chv_l = [], [], [], [], [], [], [], [], []
    for l in range(DEPTH):
        lambda_init = 0.8 - 0.6 * math.exp(-0.3 * l)
        lam = (jnp.exp(jnp.sum(lam_q1[l].astype(jnp.float32) * lam_k1[l].astype(jnp.float32)))
               - jnp.exp(jnp.sum(lam_q2[l].astype(jnp.float32) * lam_k2[l].astype(jnp.float32)))
               + lambda_init)
        mix = functools.partial(token_mixer, w_in=w_in[l], w_out=w_out[l], conv_w=conv_w[l], conv_b=conv_b[l],
                                w_a=w_a[l], b_a=b_a[l], w_x=w_x[l], b_x=b_x[l], lru_lam=lru_lam[l],
                                vn_g=vn_g[l], vn_b=vn_b[l], w_s=w_s[l], b_s=b_s[l],
                                lambda_init=lambda_init, subln_g=subln_g[l])

        def attend_prompt(q, k, v):
            return diff_attn_prompt(q, k, v, lam)

        def attend_sample(q, k, v):
            kg = cache_k[l, page_table]
            vg = cache_v[l, page_table]
            past = kg.shape[1] * kg.shape[2]
            k_all = jnp.concatenate([kg.reshape(n_dec, past, H_C, 2, HD_C).astype(k.dtype), k], axis=1)
            v_all = jnp.concatenate([vg.reshape(n_dec, past, H_C, DV_C).astype(v.dtype), v], axis=1)
            return diff_attn_block(q, past + jnp.arange(q.shape[1]), k_all, v_all, lam)

        y, h_p, c_p, _, k_p, v_p = mix(xp, jnp.zeros((n_prompt, CONV_W - 1, W_A), xp.dtype),
                                       jnp.zeros((n_prompt, W_A), jnp.float32), attend_prompt)
        xp = layer_norm(ALPHA * xp + y, ln1_g[l], ln1_b[l])
        xp = layer_norm(ALPHA * xp + swiglu(xp, w_gate[l], w_up[l], w_down[l]), ln2_g[l], ln2_b[l])
        y, h_s, c_s, vn_s, k_s, v_s = mix(xs, state_conv[l], state_lru_h[l], attend_sample)
        xs = layer_norm(ALPHA * xs + y, ln1_g[l], ln1_b[l])
        xs = layer_norm(ALPHA * xs + swiglu(xs, w_gate[l], w_up[l], w_down[l]), ln2_g[l], ln2_b[l])

        kp_l.append(k_p); vp_l.append(v_p); ks_l.append(k_s); vs_l.append(v_s)
        hp_l.append(h_p); hs_l.append(h_s); cp_l.append(c_p); cs_l.append(c_s); chv_l.append(vn_s)

    k_prompt = jnp.stack(kp_l)
    v_prompt = jnp.stack(vp_l)
    k_sample = jnp.stack(ks_l)
    v_sample = jnp.stack(vs_l)
    h_prompt = jnp.stack(hp_l)
    h_sample = jnp.stack(hs_l)
    conv_prompt = jnp.stack(cp_l)
    conv_sample = jnp.stack(cs_l)
    chunk_v_sample = jnp.stack(chv_l)
    return (xp, xs, k_prompt, v_prompt, k_sample, v_sample, h_prompt, h_sample, conv_prompt, conv_sample, chunk_v_sample)
```

```python
import functools
import math

import jax
import jax.numpy as jnp
from jax import lax
from jax.experimental import pallas as pl
from jax.experimental.pallas import tpu as pltpu

F32 = jnp.float32
BF16 = jnp.bfloat16

D_MODEL = 2048
W_A = 512
H_A = 4
BW_A = 128
CONV_W = 4
RG_C = 8.0
W_B = 512
G_B = 4
GW_B = 128
CHUNK = 128
W_C = 1024
H_C = 4
HD_C = 128
DV_C = 256
W_IN = 2 * W_A + 2 * W_B + 3 * W_C
D_FF = 5632
LN_EPS = 1e-5
NEG_INF = -1e30
QK_SCALE = HD_C ** -0.5

SUBLANES = 8
LANES = 128
COL_TILE = 512
VMEM_LIMIT = 56 * 1024 * 1024


def _cparams(sem):
    return pltpu.CompilerParams(dimension_semantics=sem, vmem_limit_bytes=VMEM_LIMIT)


def _layer_norm(r, g, b):
    mu = jnp.mean(r, axis=-1, keepdims=True)
    d = r - mu
    var = jnp.mean(d * d, axis=-1, keepdims=True)
    return d * lax.rsqrt(var + LN_EPS) * g + b


def _softplus(x):
    return jnp.maximum(x, 0.0) + jnp.log1p(jnp.exp(-jnp.abs(x)))


def _expm1(x):
    u = jnp.exp(x)
    um1 = u - 1.0
    small = jnp.abs(x) < 0.5
    exact_one = um1 == 0.0
    log_u = jnp.log(jnp.where(small & jnp.logical_not(exact_one), u, 2.0))
    return jnp.where(small, jnp.where(exact_one, x, um1 * x / log_u), um1)


def _diff_lambda(lq1, lk1, lq2, lk2, lambda_init):
    s1 = jnp.sum(lq1 * lk1, axis=-1, keepdims=True)
    s2 = jnp.sum(lq2 * lk2, axis=-1, keepdims=True)
    return jnp.exp(s1) - jnp.exp(s2) + lambda_init


def _inproj_kernel(x_ref, w_ref, za_ref, qkv_ref, kf_ref, vf_ref, xb_ref):
    j = pl.program_id(1)

    @pl.when(j == 0)
    def _():
        xb_ref[...] = x_ref[...].astype(BF16)

    z = jnp.dot(xb_ref[...], w_ref[...], preferred_element_type=F32)

    @pl.when(j < 4)
    def _():
        za_ref[...] = z

    @pl.when((j >= 4) & (j < 6))
    def _():
        qkv_ref[...] = (z * QK_SCALE).astype(qkv_ref.dtype)

    @pl.when(j >= 6)
    def _():
        qkv_ref[...] = z.astype(qkv_ref.dtype)

    @pl.when((j >= 6) & (j < 8))
    def _():
        kf_ref[...] = z

    @pl.when(j >= 8)
    def _():
        vf_ref[...] = z


def _inproj(x, w_in_b, l, tm, qkv_dtype):
    m = x.shape[0]
    nj = W_IN // COL_TILE
    return pl.pallas_call(
        _inproj_kernel,
        grid=(m // tm, nj),
        in_specs=[
            pl.BlockSpec((tm, D_MODEL), lambda i, j: (i, 0)),
            pl.BlockSpec((None, D_MODEL, COL_TILE), lambda i, j: (l, 0, j)),
        ],
        out_specs=[
            pl.BlockSpec((tm, COL_TILE), lambda i, j: (i, jnp.minimum(j, 3))),
            pl.BlockSpec((tm, COL_TILE), lambda i, j: (i, jnp.clip(j - 4, 0, 5))),
            pl.BlockSpec((tm, COL_TILE), lambda i, j: (i, jnp.clip(j - 6, 0, 1))),
            pl.BlockSpec((tm, COL_TILE), lambda i, j: (i, jnp.clip(j - 8, 0, 1))),
        ],
        out_shape=[
            jax.ShapeDtypeStruct((m, 4 * COL_TILE), F32),
            jax.ShapeDtypeStruct((m, 3 * W_C), qkv_dtype),
            jax.ShapeDtypeStruct((m, W_C), F32),
            jax.ShapeDtypeStruct((m, W_C), F32),
        ],
        scratch_shapes=[pltpu.VMEM((tm, D_MODEL), BF16)],
        compiler_params=_cparams(("arbitrary", "arbitrary")),
        name="inproj",
    )(x, w_in_b)


def _shift_rows(x, s, fill):
    n, w = x.shape
    if s % SUBLANES == 0:
        return jnp.concatenate([jnp.full((s, w), fill, x.dtype), x[:n - s]], axis=0)
    row = lax.broadcasted_iota(jnp.int32, x.shape, 0)
    return jnp.where(row < s, fill, pltpu.roll(x, s, 0))


def _block_gate(xcb, w_ref, b_ref):
    parts = [jnp.dot(xcb[:, h * BW_A:(h + 1) * BW_A], w_ref[h], preferred_element_type=F32)
             for h in range(H_A)]
    return jax.nn.sigmoid(jnp.concatenate(parts, axis=-1) + b_ref[...])


def _rglru_coeffs(xc, wa_ref, ba_ref, wx_ref, bx_ref, lam_ref):
    xcb = xc.astype(BF16)
    r = _block_gate(xcb, wa_ref, ba_ref)
    gi = _block_gate(xcb, wx_ref, bx_ref)
    log_a = -RG_C * r * _softplus(-lam_ref[...])
    a = jnp.exp(log_a)
    b = jnp.sqrt(-_expm1(2.0 * log_a)) * (gi * xc)
    return a, b


def _mix_a_kernel(xa_ref, ga_ref, cbuf_ref, h0_ref, cw_ref, cb_ref, wa_ref, ba_ref, wx_ref, bx_ref,
                  lam_ref, ya_ref, hl_ref, cout_ref, tail_ref, hc_ref):
    t = pl.program_id(1)
    tt = xa_ref.shape[0]

    @pl.when(t == 0)
    def _():
        tail_ref[...] = cbuf_ref[...]
        hc_ref[...] = h0_ref[...]

    xa = xa_ref[...]
    tail = tail_ref[...]
    row8 = lax.broadcasted_iota(jnp.int32, tail.shape, 0)

    def delayed(k):
        r = pltpu.roll(xa, k, 0)
        first = jnp.where(row8 < k, pltpu.roll(tail, k, 0), r[0:SUBLANES])
        return jnp.concatenate([first, r[SUBLANES:]], axis=0)

    cw = cw_ref[...]
    xc = cb_ref[...] + cw[3:4] * xa
    for k in range(1, CONV_W):
        xc = xc + cw[CONV_W - 1 - k:CONV_W - k] * delayed(k)
    tail_ref[...] = xa[tt - SUBLANES:tt]

    a, b = _rglru_coeffs(xc, wa_ref, ba_ref, wx_ref, bx_ref, lam_ref)
    s = 1
    while s < tt:
        b = b + a * _shift_rows(b, s, 0.0)
        a = a * _shift_rows(a, s, 1.0)
        s *= 2
    h = a * hc_ref[...] + b
    hc_ref[...] = h[tt - 1:tt]
    ya_ref[...] = (h * jax.nn.gelu(ga_ref[...])).astype(ya_ref.dtype)

    @pl.when(t == pl.num_programs(1) - 1)
    def _():
        hl_ref[...] = h[tt - 1:tt]
        cout_ref[...] = xa[tt - SUBLANES:tt]


def _mix_a(za, cbuf8, h0, prm, l, nb, tt):
    m = za.shape[0]
    nt = m // nb // tt
    vec = lambda name: pl.BlockSpec((None, 1, W_A), lambda b, t: (l, 0, 0))
    blk = pl.BlockSpec((None, H_A, BW_A, BW_A), lambda b, t: (l, 0, 0, 0))
    return pl.pallas_call(
        _mix_a_kernel,
        grid=(nb, nt),
        in_specs=[
            pl.BlockSpec((tt, W_A), lambda b, t: (b * nt + t, 0)),
            pl.BlockSpec((tt, W_A), lambda b, t: (b * nt + t, 1)),
            pl.BlockSpec((None, SUBLANES, W_A), lambda b, t: (b, 0, 0)),
            pl.BlockSpec((None, 1, W_A), lambda b, t: (b, 0, 0)),
            pl.BlockSpec((None, CONV_W, W_A), lambda b, t: (l, 0, 0)),
            vec("conv_b"), blk, vec("b_a"), blk, vec("b_x"), vec("lru_lam"),
        ],
        out_specs=[
            pl.BlockSpec((tt, W_A), lambda b, t: (b * nt + t, 0)),
            pl.BlockSpec((None, 1, W_A), lambda b, t: (b, 0, 0)),
            pl.BlockSpec((None, SUBLANES, W_A), lambda b, t: (b, 0, 0)),
        ],
        out_shape=[
            jax.ShapeDtypeStruct((m, W_A), BF16),
            jax.ShapeDtypeStruct((nb, 1, W_A), F32),
            jax.ShapeDtypeStruct((nb, SUBLANES, W_A), F32),
        ],
        scratch_shapes=[pltpu.VMEM((SUBLANES, W_A), F32), pltpu.VMEM((1, W_A), F32)],
        compiler_params=_cparams(("arbitrary", "arbitrary")),
        name="mix_a",
    )(za, za, cbuf8, h0, prm["conv_w"], prm["conv_b"], prm["w_a"], prm["b_a"], prm["w_x"], prm["b_x"],
      prm["lru_lam"])


def _mix_b_kernel(ub_ref, vb_ref, ws_ref, bs_ref, g_ref, b_ref, yb_ref):
    tt = ub_ref.shape[0]
    vn = _layer_norm(jax.nn.gelu(vb_ref[...]), g_ref[...], b_ref[...]).astype(BF16)
    row = lax.broadcasted_iota(jnp.int32, (CHUNK, CHUNK), 0)
    col = lax.broadcasted_iota(jnp.int32, (CHUNK, CHUNK), 1)
    wt = [jnp.where(col <= row, ws_ref[g], 0.0).astype(BF16) for g in range(G_B)]
    bias = bs_ref[...]
    chunks = []
    for c in range(tt // CHUNK):
        vc = vn[c * CHUNK:(c + 1) * CHUNK]
        parts = [jnp.dot(wt[g], vc[:, g * GW_B:(g + 1) * GW_B], preferred_element_type=F32)
                 for g in range(G_B)]
        chunks.append(jnp.concatenate(parts, axis=-1) + bias)
    mix = jnp.concatenate(chunks, axis=0)
    yb_ref[...] = (jax.nn.gelu(ub_ref[...]) * mix).astype(yb_ref.dtype)


def _mix_b(za, prm, l, tt):
    m = za.shape[0]
    return pl.pallas_call(
        _mix_b_kernel,
        grid=(m // tt,),
        in_specs=[
            pl.BlockSpec((tt, W_B), lambda i: (i, 2)),
            pl.BlockSpec((tt, W_B), lambda i: (i, 3)),
            pl.BlockSpec((None, G_B, CHUNK, CHUNK), lambda i: (l, 0, 0, 0)),
            pl.BlockSpec((None, CHUNK, W_B), lambda i: (l, 0, 0)),
            pl.BlockSpec((None, 1, W_B), lambda i: (l, 0, 0)),
            pl.BlockSpec((None, 1, W_B), lambda i: (l, 0, 0)),
        ],
        out_specs=pl.BlockSpec((tt, W_B), lambda i: (i, 0)),
        out_shape=jax.ShapeDtypeStruct((m, W_B), BF16),
        compiler_params=_cparams(("arbitrary",)),
        name="mix_b",
    )(za, za, prm["w_s"], prm["bs_rows"], prm["vn_g"], prm["vn_b"])


def _attn_kernel(q_ref, k_ref, v_ref, lq1_ref, lk1_ref, lq2_ref, lk2_ref, g_ref, o_ref,
                 m_ref, l_ref, acc_ref, *, lambda_init):
    qi = pl.program_id(2)
    ki = pl.program_id(3)
    tq = q_ref.shape[0]
    tk = k_ref.shape[0]

    @pl.when(ki == 0)
    def _():
        m_ref[...] = jnp.full(m_ref.shape, NEG_INF, F32)
        l_ref[...] = jnp.zeros(l_ref.shape, F32)
        acc_ref[...] = jnp.zeros(acc_ref.shape, F32)

    @pl.when(ki <= qi)
    def _():
        q = q_ref[...]
        k = k_ref[...]
        v = v_ref[...]
        row = lax.broadcasted_iota(jnp.int32, (tq, tk), 0) + qi * tq
        col = lax.broadcasted_iota(jnp.int32, (tq, tk), 1) + ki * tk
        causal = col <= row
        for c in range(2):
            s = lax.dot_general(q[:, c * HD_C:(c + 1) * HD_C], k[:, c * HD_C:(c + 1) * HD_C],
                                (((1,), (1,)), ((), ())), preferred_element_type=F32)
            s = jnp.where(causal, s, NEG_INF)
            m_prev = m_ref[c]
            m_new = jnp.maximum(m_prev, jnp.max(s, axis=-1, keepdims=True))
            alpha = jnp.exp(m_prev - m_new)
            p = jnp.exp(s - m_new)
            l_ref[c] = alpha * l_ref[c] + jnp.sum(p, axis=-1, keepdims=True)
            acc_ref[c] = alpha * acc_ref[c] + jnp.dot(p.astype(BF16), v, preferred_element_type=F32)
            m_ref[c] = m_new

    @pl.when(ki == qi)
    def _():
        lam = _diff_lambda(lq1_ref[...], lk1_ref[...], lq2_ref[...], lk2_ref[...], lambda_init)
        o = acc_ref[0] / l_ref[0] - lam * (acc_ref[1] / l_ref[1])
        ms = jnp.mean(o * o, axis=-1, keepdims=True)
        o_ref[...] = (o * lax.rsqrt(ms + LN_EPS) * g_ref[...] * (1.0 - lambda_init)).astype(o_ref.dtype)


def _attn(qkv, prm, l, nb, tq, lambda_init):
    m = qkv.shape[0]
    nq = m // nb // tq
    lamspec = pl.BlockSpec((None, 1, HD_C), lambda b, h, qi, ki: (l, 0, 0))
    return pl.pallas_call(
        functools.partial(_attn_kernel, lambda_init=lambda_init),
        grid=(nb, H_C, nq, nq),
        in_specs=[
            pl.BlockSpec((tq, DV_C), lambda b, h, qi, ki: (b * nq + qi, h)),
            pl.BlockSpec((tq, DV_C), lambda b, h, qi, ki: (b * nq + jnp.minimum(ki, qi), H_C + h)),
            pl.BlockSpec((tq, DV_C), lambda b, h, qi, ki: (b * nq + jnp.minimum(ki, qi), 2 * H_C + h)),
            lamspec, lamspec, lamspec, lamspec,
            pl.BlockSpec((None, 1, DV_C), lambda b, h, qi, ki: (l, 0, 0)),
        ],
        out_specs=pl.BlockSpec((tq, DV_C), lambda b, h, qi, ki: (b * nq + qi, h)),
        out_shape=jax.ShapeDtypeStruct((m, W_C), BF16),
        scratch_shapes=[
            pltpu.VMEM((2, tq, 1), F32),
            pltpu.VMEM((2, tq, 1), F32),
            pltpu.VMEM((2, tq, DV_C), F32),
        ],
        compiler_params=_cparams(("arbitrary",) * 4),
        name="attn",
    )(qkv, qkv, qkv, prm["lam_q1"], prm["lam_k1"], prm["lam_q2"], prm["lam_k2"], prm["subln_g"])


def _outproj_kernel(ya_ref, yb_ref, o_ref, x_ref, w_ref, g_ref, b_ref, out_ref, *, alpha):
    cat = jnp.concatenate([ya_ref[...].astype(BF16), yb_ref[...].astype(BF16), o_ref[...].astype(BF16)],
                          axis=-1)
    y = jnp.dot(cat, w_ref[...], preferred_element_type=F32)
    out_ref[...] = _layer_norm(alpha * x_ref[...] + y, g_ref[...], b_ref[...])


def _outproj(ya, yb, o, x, prm, l, tm, alpha):
    m = x.shape[0]
    vec = pl.BlockSpec((None, 1, D_MODEL), lambda i: (l, 0, 0))
    return pl.pallas_call(
        functools.partial(_outproj_kernel, alpha=alpha),
        grid=(m // tm,),
        in_specs=[
            pl.BlockSpec((tm, W_A), lambda i: (i, 0)),
            pl.BlockSpec((tm, W_B), lambda i: (i, 0)),
            pl.BlockSpec((tm, W_C), lambda i: (i, 0)),
            pl.BlockSpec((tm, D_MODEL), lambda i: (i, 0)),
            pl.BlockSpec((None, D_MODEL, D_MODEL), lambda i: (l, 0, 0)),
            vec, vec,
        ],
        out_specs=pl.BlockSpec((tm, D_MODEL), lambda i: (i, 0)),
        out_shape=jax.ShapeDtypeStruct((m, D_MODEL), F32),
        compiler_params=_cparams(("arbitrary",)),
        name="outproj",
    )(ya, yb, o, x, prm["w_out"], prm["ln1_g"], prm["ln1_b"])


def _swiglu_kernel(x_ref, wg_ref, wu_ref, wd_ref, g_ref, b_ref, out_ref, xb_ref, acc_ref, *, alpha):
    f = pl.program_id(1)

    @pl.when(f == 0)
    def _():
        xb_ref[...] = x_ref[...].astype(BF16)
        acc_ref[...] = jnp.zeros(acc_ref.shape, F32)

    xb = xb_ref[...]
    hg = jnp.dot(xb, wg_ref[...], preferred_element_type=F32)
    hu = jnp.dot(xb, wu_ref[...], preferred_element_type=F32)
    h = (hg * jax.nn.sigmoid(hg)) * hu
    acc_ref[...] += jnp.dot(h.astype(BF16), wd_ref[...], preferred_element_type=F32)

    @pl.when(f == pl.num_programs(1) - 1)
    def _():
        out_ref[...] = _layer_norm(alpha * x_ref[...] + acc_ref[...], g_ref[...], b_ref[...])


def _swiglu(x, prm, l, tm, tf, alpha):
    m = x.shape[0]
    vec = pl.BlockSpec((None, 1, D_MODEL), lambda i, f: (l, 0, 0))
    return pl.pallas_call(
        functools.partial(_swiglu_kernel, alpha=alpha),
        grid=(m // tm, D_FF // tf),
        in_specs=[
            pl.BlockSpec((tm, D_MODEL), lambda i, f: (i, 0)),
            pl.BlockSpec((None, D_MODEL, tf), lambda i, f: (l, 0, f)),
            pl.BlockSpec((None, D_MODEL, tf), lambda i, f: (l, 0, f)),
            pl.BlockSpec((None, tf, D_MODEL), lambda i, f: (l, f, 0)),
            vec, vec,
        ],
        out_specs=pl.BlockSpec((tm, D_MODEL), lambda i, f: (i, 0)),
        out_shape=jax.ShapeDtypeStruct((m, D_MODEL), F32),
        scratch_shapes=[pltpu.VMEM((tm, D_MODEL), BF16), pltpu.VMEM((tm, D_MODEL), F32)],
        compiler_params=_cparams(("arbitrary", "arbitrary")),
        name="swiglu",
    )(x, prm["w_gate"], prm["w_up"], prm["w_down"], prm["ln2_g"], prm["ln2_b"])


def _sample_mix_kernel(za_ref, cbuf_ref, h0_ref, cw_ref, cb_ref, wa_ref, ba_ref, wx_ref, bx_ref, lam_ref,
                       vng_ref, vnb_ref, w4_ref, b4_ref,
                       ya_ref, yb_ref, hl_ref, cout_ref, vn_ref):
    nt = vn_ref.shape[0]
    nb = h0_ref.shape[0]

    lanes = za_ref.shape[1]
    zc = 4 * COL_TILE // lanes
    gc = COL_TILE // lanes

    def rows(t, grp):
        return jnp.concatenate(
            [za_ref[pl.ds(t * zc + grp * gc + c, nb, stride=nt * zc), :] for c in range(gc)], axis=-1)

    def put_rows(ref, t, val):
        for c in range(gc):
            ref[pl.ds(t * gc + c, nb, stride=nt * gc), :] = val[:, c * lanes:(c + 1) * lanes]

    xa = [rows(t, 0) for t in range(nt)]
    xp = [cbuf_ref[j] for j in range(CONV_W - 1)] + xa
    cw = cw_ref[...]
    xc = []
    for t in range(nt):
        acc = cb_ref[...] + cw[0:1] * xp[t]
        for j in range(1, CONV_W):
            acc = acc + cw[j:j + 1] * xp[t + j]
        xc.append(acc)
    a, b = _rglru_coeffs(jnp.concatenate(xc, axis=0), wa_ref, ba_ref, wx_ref, bx_ref, lam_ref)
    h = h0_ref[...]
    for t in range(nt):
        h = a[t * nb:(t + 1) * nb] * h + b[t * nb:(t + 1) * nb]
        put_rows(ya_ref, t, h * jax.nn.gelu(rows(t, 1)))
    hl_ref[...] = h
    for j in range(CONV_W - 1):
        cout_ref[j] = xp[nt + j]

    vn = []
    for t in range(nt):
        v = _layer_norm(jax.nn.gelu(rows(t, 3)), vng_ref[...], vnb_ref[...])
        vn_ref[t] = v
        vn.append(v)
    for t in range(nt):
        mix = b4_ref[t:t + 1]
        for s in range(t + 1):
            mix = mix + w4_ref[t, s:s + 1] * vn[s]
        put_rows(yb_ref, t, jax.nn.gelu(rows(t, 2)) * mix)


def _sample_mix(za, cbuf, h0, prm, l, nb, nt):
    m = nb * nt
    full = lambda shape: pl.BlockSpec(shape, lambda i: (0,) * len(shape))
    lsel = lambda shape: pl.BlockSpec((None,) + shape, lambda i: (l,) + (0,) * len(shape))
    return pl.pallas_call(
        _sample_mix_kernel,
        grid=(1,),
        in_specs=[
            full((m * 4 * COL_TILE // LANES, LANES)),
            lsel((CONV_W - 1, nb, W_A)),
            lsel((nb, W_A)),
            lsel((CONV_W, W_A)), lsel((1, W_A)),
            lsel((H_A, BW_A, BW_A)), lsel((1, W_A)),
            lsel((H_A, BW_A, BW_A)), lsel((1, W_A)),
            lsel((1, W_A)),
            lsel((1, W_B)), lsel((1, W_B)),
            lsel((nt, nt, W_B)), lsel((nt, W_B)),
        ],
        out_specs=[
            full((m * W_A // LANES, LANES)), full((m * W_B // LANES, LANES)), full((nb, W_A)),
            full((CONV_W - 1, nb, W_A)), full((nt, nb, W_B)),
        ],
        out_shape=[
            jax.ShapeDtypeStruct((m * W_A // LANES, LANES), F32),
            jax.ShapeDtypeStruct((m * W_B // LANES, LANES), F32),
            jax.ShapeDtypeStruct((nb, W_A), F32),
            jax.ShapeDtypeStruct((CONV_W - 1, nb, W_A), F32),
            jax.ShapeDtypeStruct((nt, nb, W_B), F32),
        ],
        compiler_params=_cparams(("arbitrary",)),
        name="sample_mix",
    )(za, cbuf, h0, prm["conv_w"], prm["conv_b"], prm["w_a"], prm["b_a"], prm["w_x"], prm["b_x"],
      prm["lru_lam"], prm["vn_g"], prm["vn_b"], prm["w4"], prm["b4"])


def _expand_queries(q):
    hc = lax.broadcasted_iota(jnp.int32, (2 * H_C, W_C), 0)
    blk = lax.broadcasted_iota(jnp.int32, (2 * H_C, W_C), 1) // HD_C
    keep = hc == blk
    return jnp.concatenate(
        [jnp.where(keep, jnp.broadcast_to(q[t:t + 1], (2 * H_C, W_C)), 0.0) for t in range(q.shape[0])],
        axis=0)


def _paged_attn_kernel(pt_ref, q_ref, kn_ref, vn_ref, lq1_ref, lk1_ref, lq2_ref, lk2_ref, g_ref, *rest,
                       pages_per_step, lambda_init):
    k_refs = rest[:pages_per_step]
    v_refs = rest[pages_per_step:2 * pages_per_step]
    o_ref, qe_ref, m_ref, l_ref, acc_ref = rest[2 * pages_per_step:]
    j = pl.program_id(1)
    nt = q_ref.shape[0]
    page = k_refs[0].shape[0]

    @pl.when(j == 0)
    def _():
        qe_ref[...] = _expand_queries(q_ref[...]).astype(BF16)
        m_ref[...] = jnp.full(m_ref.shape, NEG_INF, F32)
        l_ref[...] = jnp.zeros(l_ref.shape, F32)
        acc_ref[...] = jnp.zeros(acc_ref.shape, F32)

    qe = qe_ref[...]
    s = jnp.concatenate(
        [lax.dot_general(qe, k_refs[p][...].astype(BF16), (((1,), (1,)), ((), ())),
                         preferred_element_type=F32) for p in range(pages_per_step)], axis=-1)
    m_prev = m_ref[...]
    m_new = jnp.maximum(m_prev, jnp.max(s, axis=-1, keepdims=True))
    alpha = jnp.exp(m_prev - m_new)
    p = jnp.exp(s - m_new)
    l_ref[...] = alpha * l_ref[...] + jnp.sum(p, axis=-1, keepdims=True)
    pb = p.astype(BF16)
    pv = jnp.dot(pb[:, 0:page], v_refs[0][...].astype(BF16), preferred_element_type=F32)
    for i in range(1, pages_per_step):
        pv = pv + jnp.dot(pb[:, i * page:(i + 1) * page], v_refs[i][...].astype(BF16),
                          preferred_element_type=F32)
    acc_ref[...] = alpha * acc_ref[...] + pv
    m_ref[...] = m_new

    @pl.when(j == pl.num_programs(1) - 1)
    def _():
        qe32 = _expand_queries(q_ref[...])
        trow = lax.broadcasted_iota(jnp.int32, (nt * 2 * H_C, 1), 0) // (2 * H_C)
        kn = kn_ref[...]
        vn = vn_ref[...]
        sc = [jnp.where(trow >= t, jnp.sum(qe32 * kn[t:t + 1], axis=-1, keepdims=True), NEG_INF)
              for t in range(nt)]
        m_prev = m_ref[...]
        m_new = m_prev
        for t in range(nt):
            m_new = jnp.maximum(m_new, sc[t])
        alpha = jnp.exp(m_prev - m_new)
        lsum = alpha * l_ref[...]
        acc = alpha * acc_ref[...]
        for t in range(nt):
            pt = jnp.exp(sc[t] - m_new)
            lsum = lsum + pt
            acc = acc + pt * vn[t:t + 1]
        accn = acc / lsum
        lam = _diff_lambda(lq1_ref[...], lk1_ref[...], lq2_ref[...], lk2_ref[...], lambda_init)
        hc = lax.broadcasted_iota(jnp.int32, (nt * 2 * H_C, 1), 0) % (2 * H_C)
        outs = []
        for h in range(H_C):
            w = jnp.where(hc == 2 * h, 1.0, jnp.where(hc == 2 * h + 1, -lam, 0.0))
            o = jnp.sum((accn[:, h * DV_C:(h + 1) * DV_C] * w).reshape(nt, 2 * H_C, DV_C), axis=1)
            ms = jnp.mean(o * o, axis=-1, keepdims=True)
            outs.append(o * lax.rsqrt(ms + LN_EPS) * g_ref[...] * (1.0 - lambda_init))
        o_ref[...] = jnp.concatenate(outs, axis=-1)


def _paged_attn(page_table, qkv3, cache_k, cache_v, prm, l, pages_per_step, lambda_init):
    nb, nt, _ = qkv3.shape
    n_pages = page_table.shape[0] // nb
    page = cache_k.shape[2]
    lamspec = pl.BlockSpec((None, 1, HD_C), lambda b, j, pt: (l, 0, 0))

    def page_spec(p):
        return pl.BlockSpec((None, None, page, W_C),
                            lambda b, j, pt: (l, pt[b * n_pages + j * pages_per_step + p], 0, 0))

    grid_spec = pltpu.PrefetchScalarGridSpec(
        num_scalar_prefetch=1,
        grid=(nb, n_pages // pages_per_step),
        in_specs=[
            pl.BlockSpec((None, nt, W_C), lambda b, j, pt: (b, 0, 0)),
            pl.BlockSpec((None, nt, W_C), lambda b, j, pt: (b, 0, 1)),
            pl.BlockSpec((None, nt, W_C), lambda b, j, pt: (b, 0, 2)),
            lamspec, lamspec, lamspec, lamspec,
            pl.BlockSpec((None, 1, DV_C), lambda b, j, pt: (l, 0, 0)),
        ] + [page_spec(p) for p in range(pages_per_step)] * 2,
        out_specs=pl.BlockSpec((None, nt, W_C), lambda b, j, pt: (b, 0, 0)),
        scratch_shapes=[
            pltpu.VMEM((nt * 2 * H_C, W_C), BF16),
            pltpu.VMEM((nt * 2 * H_C, 1), F32),
            pltpu.VMEM((nt * 2 * H_C, 1), F32),
            pltpu.VMEM((nt * 2 * H_C, W_C), F32),
        ],
    )
    return pl.pallas_call(
        functools.partial(_paged_attn_kernel, pages_per_step=pages_per_step, lambda_init=lambda_init),
        grid_spec=grid_spec,
        out_shape=jax.ShapeDtypeStruct((nb, nt, W_C), F32),
        compiler_params=_cparams(("arbitrary", "arbitrary")),
        name="paged_attn",
    )(page_table, qkv3, qkv3, qkv3, prm["lam_q1"], prm["lam_k1"], prm["lam_q2"], prm["lam_k2"],
      prm["subln_g"], *([cache_k] * pages_per_step), *([cache_v] * pages_per_step))


def _largest_tile(n, cap, mult):
    t = min(n, cap)
    while n % t or t % mult:
        t -= 1
    return t


def kernel(x_prompt, x_sample, cache_k, cache_v, state_lru_h, state_conv, page_table, w_in, w_out, conv_w,
           conv_b, w_a, b_a, w_x, b_x, lru_lam, vn_g, vn_b, w_s, b_s, lam_q1, lam_k1, lam_q2, lam_k2,
           subln_g, ln1_g, ln1_b, ln2_g, ln2_b, w_gate, w_up, w_down):
    depth = w_in.shape[0]
    nbp, seq, _ = x_prompt.shape
    nbs, nts, _ = x_sample.shape
    n_pool, page = cache_k.shape[1], cache_k.shape[2]
    n_pages = page_table.shape[1]
    alpha = (2.0 * depth) ** 0.25
    mp, ms = nbp * seq, nbs * nts

    row3 = lambda a: a.reshape(depth, 1, a.shape[-1])
    prm = {
        "w_in": w_in.astype(BF16), "w_out": w_out.astype(BF16),
        "w_gate": w_gate.astype(BF16), "w_up": w_up.astype(BF16), "w_down": w_down.astype(BF16),
        "w_a": w_a.astype(BF16), "w_x": w_x.astype(BF16),
        "conv_w": conv_w, "conv_b": row3(conv_b), "b_a": row3(b_a), "b_x": row3(b_x),
        "lru_lam": row3(lru_lam), "vn_g": row3(vn_g), "vn_b": row3(vn_b), "w_s": w_s,
        "bs_rows": jnp.repeat(jnp.transpose(b_s, (0, 2, 1)), GW_B, axis=-1),
        "w4": jnp.repeat(jnp.transpose(w_s[:, :, :nts, :nts], (0, 2, 3, 1)), GW_B, axis=-1),
        "b4": jnp.repeat(jnp.transpose(b_s[:, :, :nts], (0, 2, 1)), GW_B, axis=-1),
        "lam_q1": row3(lam_q1), "lam_k1": row3(lam_k1), "lam_q2": row3(lam_q2), "lam_k2": row3(lam_k2),
        "subln_g": row3(subln_g), "ln1_g": row3(ln1_g), "ln1_b": row3(ln1_b),
        "ln2_g": row3(ln2_g), "ln2_b": row3(ln2_b),
    }
    ck = cache_k.reshape(depth, n_pool, page, W_C)
    cv = cache_v.reshape(depth, n_pool, page, W_C)
    pt_flat = page_table.reshape(-1)
    sconv = jnp.transpose(state_conv, (0, 2, 1, 3))
    zero_cbuf = jnp.zeros((nbp, SUBLANES, W_A), F32)
    zero_h = jnp.zeros((nbp, 1, W_A), F32)

    tm_p = _largest_tile(mp, 512, CHUNK)
    tt_a = _largest_tile(seq, 256, SUBLANES)
    tq = _largest_tile(seq, 512, CHUNK)
    tf = _largest_tile(D_FF, 512, 128)
    pps = _largest_tile(n_pages, 8, 1)

    xp = x_prompt.reshape(mp, D_MODEL)
    xs = x_sample.reshape(ms, D_MODEL)
    outs = [[] for _ in range(9)]
    for l in range(depth):
        lambda_init = 0.8 - 0.6 * math.exp(-0.3 * l)
        za, qkv, kf, vf = _inproj(xp, prm["w_in"], l, tm_p, BF16)
        ya, h_p, c_p = _mix_a(za, zero_cbuf, zero_h, prm, l, nbp, tt_a)
        yb = _mix_b(za, prm, l, tm_p)
        o = _attn(qkv, prm, l, nbp, tq, lambda_init)
        xp = _outproj(ya, yb, o, xp, prm, l, tm_p, alpha)
        xp = _swiglu(xp, prm, l, tm_p, tf, alpha)
        za_s, qkv_s, kf_s, vf_s = _inproj(xs, prm["w_in"], l, ms, F32)
        ya_s, yb_s, h_s, c_s, vn_s = _sample_mix(za_s.reshape(-1, LANES), sconv, state_lru_h, prm, l, nbs,
                                                  nts)
        ya_s, yb_s = ya_s.reshape(ms, W_A), yb_s.reshape(ms, W_B)
        o_s = _paged_attn(pt_flat, qkv_s.reshape(nbs, nts, 3 * W_C), ck, cv, prm, l, pps, lambda_init)
        xs = _outproj(ya_s, yb_s, o_s.reshape(ms, W_C), xs, prm, l, ms, alpha)
        xs = _swiglu(xs, prm, l, ms, tf, alpha)

        for lst, val in zip(outs, (kf, vf, kf_s, vf_s, h_p, h_s, c_p, c_s, vn_s)):
            lst.append(val)

    kp, vp, ksm, vsm, hp, hs, cp, cs, chv = [jnp.stack(v) for v in outs]
    return (
        xp.reshape(nbp, seq, D_MODEL),
        xs.reshape(nbs, nts, D_MODEL),
        kp.reshape(depth, nbp, seq, H_C, 2, HD_C),
        vp.reshape(depth, nbp, seq, H_C, DV_C),
        ksm.reshape(depth, nbs, nts, H_C, 2, HD_C),
        vsm.reshape(depth, nbs, nts, H_C, DV_C),
        hp.reshape(depth, nbp, W_A),
        hs,
        cp[:, :, SUBLANES - (CONV_W - 1):, :],
        jnp.transpose(cs, (0, 2, 1, 3)),
        jnp.transpose(chv, (0, 2, 1, 3)),
    )
```

```python
import functools
import math

import jax
import jax.numpy as jnp
from jax import lax
from jax.experimental import pallas as pl
from jax.experimental.pallas import tpu as pltpu

F32 = jnp.float32
BF16 = jnp.bfloat16

D_MODEL = 2048
W_A = 512
H_A = 4
BW_A = 128
CONV_W = 4
RG_C = 8.0
W_B = 512
G_B = 4
GW_B = 128
CHUNK = 128
W_C = 1024
H_C = 4
HD_C = 128
DV_C = 256
W_IN = 2 * W_A + 2 * W_B + 3 * W_C
D_FF = 5632
LN_EPS = 1e-5
NEG_INF = -1e30
QK_SCALE = HD_C ** -0.5 * math.log2(math.e)

SUBLANES = 8
LANES = 128
COL_TILE = 1024
MIX_W = 512
HEAD_ROWS = W_C // LANES
VMEM_LIMIT = 56 * 1024 * 1024


def _cparams(sem):
    return pltpu.CompilerParams(dimension_semantics=sem, vmem_limit_bytes=VMEM_LIMIT)


def _layer_norm(r, g, b):
    mu = jnp.mean(r, axis=-1, keepdims=True)
    d = r - mu
    var = jnp.mean(d * d, axis=-1, keepdims=True)
    return d * lax.rsqrt(var + LN_EPS) * g + b


def _softplus(x):
    return jnp.maximum(x, 0.0) + jnp.log1p(jnp.exp(-jnp.abs(x)))


def _expm1(x):
    u = jnp.exp(x)
    um1 = u - 1.0
    small = jnp.abs(x) < 0.5
    exact_one = um1 == 0.0
    log_u = jnp.log(jnp.where(small & jnp.logical_not(exact_one), u, 2.0))
    return jnp.where(small, jnp.where(exact_one, x, um1 * x / log_u), um1)


def _diff_lambda(lq1, lk1, lq2, lk2, lambda_init):
    s1 = jnp.sum(lq1 * lk1, axis=-1, keepdims=True)
    s2 = jnp.sum(lq2 * lk2, axis=-1, keepdims=True)
    return jnp.exp(s1) - jnp.exp(s2) + lambda_init


def _lane_tile(x, width):
    return jnp.concatenate([x] * (width // LANES), axis=-1)


def _k_piece(hc):
    return hc


def _v_piece(h, half):
    return half * H_C + h


def _inproj_kernel(x_ref, w_ref, kin_ref, vin_ref, za_ref, qkv_ref, kf_ref, vf_ref):
    del kin_ref, vin_ref
    j = pl.program_id(1)
    tm = x_ref.shape[0]
    z = jnp.dot(x_ref[...], w_ref[...], preferred_element_type=F32)

    @pl.when(j < 2)
    def _():
        za_ref[...] = z

    @pl.when(j == 2)
    def _():
        qkv_ref[...] = (z * QK_SCALE).astype(qkv_ref.dtype)

    @pl.when(j >= 3)
    def _():
        qkv_ref[...] = z.astype(qkv_ref.dtype)

    @pl.when(j == 3)
    def _():
        for hc in range(2 * H_C):
            kf_ref[pl.ds(_k_piece(hc), tm, stride=HEAD_ROWS), :] = z[:, hc * HD_C:(hc + 1) * HD_C]

    @pl.when(j == 4)
    def _():
        for h in range(H_C):
            for half in range(2):
                c0 = h * DV_C + half * LANES
                vf_ref[pl.ds(_v_piece(h, half), tm, stride=HEAD_ROWS), :] = z[:, c0:c0 + LANES]


def _inproj(xb, w_in_b, kbuf, vbuf, l, tm, qkv_dtype):
    m = xb.shape[0]
    nj = W_IN // COL_TILE
    kv_spec = pl.BlockSpec((None, tm * HEAD_ROWS, LANES), lambda i, j: (l, i, 0))
    return pl.pallas_call(
        _inproj_kernel,
        grid=(m // tm, nj),
        in_specs=[
            pl.BlockSpec((tm, D_MODEL), lambda i, j: (i, 0)),
            pl.BlockSpec((None, D_MODEL, COL_TILE), lambda i, j: (l, 0, j)),
            pl.BlockSpec(memory_space=pl.ANY),
            pl.BlockSpec(memory_space=pl.ANY),
        ],
        out_specs=[
            pl.BlockSpec((tm, COL_TILE), lambda i, j: (i, jnp.minimum(j, 1))),
            pl.BlockSpec((tm, COL_TILE), lambda i, j: (i, jnp.clip(j - 2, 0, 2))),
            kv_spec, kv_spec,
        ],
        out_shape=[
            jax.ShapeDtypeStruct((m, 4 * MIX_W), F32),
            jax.ShapeDtypeStruct((m, 3 * W_C), qkv_dtype),
            jax.ShapeDtypeStruct(kbuf.shape, F32),
            jax.ShapeDtypeStruct(vbuf.shape, F32),
        ],
        input_output_aliases={2: 2, 3: 3},
        compiler_params=_cparams(("arbitrary", "arbitrary")),
        name="inproj",
    )(xb, w_in_b, kbuf, vbuf)


def _shift_rows(x, s, fill):
    n, w = x.shape
    if s % SUBLANES == 0:
        return jnp.concatenate([jnp.full((s, w), fill, x.dtype), x[:n - s]], axis=0)
    row = lax.broadcasted_iota(jnp.int32, x.shape, 0)
    return jnp.where(row < s, fill, pltpu.roll(x, s, 0))


def _block_gate(xcb, w_ref, b_ref):
    parts = [jnp.dot(xcb[:, h * BW_A:(h + 1) * BW_A], w_ref[h], preferred_element_type=F32)
             for h in range(H_A)]
    return jax.nn.sigmoid(jnp.concatenate(parts, axis=-1) + b_ref[...])


def _rglru_coeffs(xc, wa_ref, ba_ref, wx_ref, bx_ref, lam_ref):
    xcb = xc.astype(BF16)
    r = _block_gate(xcb, wa_ref, ba_ref)
    gi = _block_gate(xcb, wx_ref, bx_ref)
    log_a = -RG_C * r * _softplus(-lam_ref[...])
    a = jnp.exp(log_a)
    b = jnp.sqrt(-_expm1(2.0 * log_a)) * (gi * xc)
    return a, b


def _mix_a_kernel(xa_ref, ga_ref, cbuf_ref, h0_ref, cw_ref, cb_ref, wa_ref, ba_ref, wx_ref, bx_ref,
                  lam_ref, ya_ref, hl_ref, cout_ref, tail_ref, hc_ref):
    t = pl.program_id(1)
    tt = xa_ref.shape[0]

    @pl.when(t == 0)
    def _():
        tail_ref[...] = cbuf_ref[...]
        hc_ref[...] = h0_ref[...]

    xa = xa_ref[...]
    tail = tail_ref[...]
    row8 = lax.broadcasted_iota(jnp.int32, tail.shape, 0)

    def delayed(k):
        r = pltpu.roll(xa, k, 0)
        first = jnp.where(row8 < k, pltpu.roll(tail, k, 0), r[0:SUBLANES])
        return jnp.concatenate([first, r[SUBLANES:]], axis=0)

    cw = cw_ref[...]
    xc = cb_ref[...] + cw[3:4] * xa
    for k in range(1, CONV_W):
        xc = xc + cw[CONV_W - 1 - k:CONV_W - k] * delayed(k)
    tail_ref[...] = xa[tt - SUBLANES:tt]

    a, b = _rglru_coeffs(xc, wa_ref, ba_ref, wx_ref, bx_ref, lam_ref)
    s = 1
    while s < tt:
        b = b + a * _shift_rows(b, s, 0.0)
        a = a * _shift_rows(a, s, 1.0)
        s *= 2
    h = a * hc_ref[...] + b
    hc_ref[...] = h[tt - 1:tt]
    ya_ref[...] = (h * jax.nn.gelu(ga_ref[...])).astype(ya_ref.dtype)

    @pl.when(t == pl.num_programs(1) - 1)
    def _():
        hl_ref[...] = h[tt - 1:tt]
        cout_ref[...] = xa[tt - SUBLANES:tt]


def _mix_a(za, cbuf8, h0, prm, l, nb, tt):
    m = za.shape[0]
    nt = m // nb // tt
    vec = pl.BlockSpec((None, 1, W_A), lambda b, t: (l, 0, 0))
    blk = pl.BlockSpec((None, H_A, BW_A, BW_A), lambda b, t: (l, 0, 0, 0))
    return pl.pallas_call(
        _mix_a_kernel,
        grid=(nb, nt),
        in_specs=[
            pl.BlockSpec((tt, W_A), lambda b, t: (b * nt + t, 0)),
            pl.BlockSpec((tt, W_A), lambda b, t: (b * nt + t, 1)),
            pl.BlockSpec((None, SUBLANES, W_A), lambda b, t: (b, 0, 0)),
            pl.BlockSpec((None, 1, W_A), lambda b, t: (b, 0, 0)),
            pl.BlockSpec((None, CONV_W, W_A), lambda b, t: (l, 0, 0)),
            vec, blk, vec, blk, vec, vec,
        ],
        out_specs=[
            pl.BlockSpec((tt, W_A), lambda b, t: (b * nt + t, 0)),
            pl.BlockSpec((None, 1, W_A), lambda b, t: (b, 0, 0)),
            pl.BlockSpec((None, SUBLANES, W_A), lambda b, t: (b, 0, 0)),
        ],
        out_shape=[
            jax.ShapeDtypeStruct((m, W_A), BF16),
            jax.ShapeDtypeStruct((nb, 1, W_A), F32),
            jax.ShapeDtypeStruct((nb, SUBLANES, W_A), F32),
        ],
        scratch_shapes=[pltpu.VMEM((SUBLANES, W_A), F32), pltpu.VMEM((1, W_A), F32)],
        compiler_params=_cparams(("arbitrary", "arbitrary")),
        name="mix_a",
    )(za, za, cbuf8, h0, prm["conv_w"], prm["conv_b"], prm["w_a"], prm["b_a"], prm["w_x"], prm["b_x"],
      prm["lru_lam"])


def _mix_b_kernel(ub_ref, vb_ref, ws_ref, bs_ref, g_ref, b_ref, yb_ref):
    tt = ub_ref.shape[0]
    vn = _layer_norm(jax.nn.gelu(vb_ref[...]), g_ref[...], b_ref[...]).astype(BF16)
    row = lax.broadcasted_iota(jnp.int32, (CHUNK, CHUNK), 0)
    col = lax.broadcasted_iota(jnp.int32, (CHUNK, CHUNK), 1)
    wt = [jnp.where(col <= row, ws_ref[g], 0.0).astype(BF16) for g in range(G_B)]
    bias = bs_ref[...]
    chunks = []
    for c in range(tt // CHUNK):
        vc = vn[c * CHUNK:(c + 1) * CHUNK]
        parts = [jnp.dot(wt[g], vc[:, g * GW_B:(g + 1) * GW_B], preferred_element_type=F32)
                 for g in range(G_B)]
        chunks.append(jnp.concatenate(parts, axis=-1) + bias)
    mix = jnp.concatenate(chunks, axis=0)
    yb_ref[...] = (jax.nn.gelu(ub_ref[...]) * mix).astype(yb_ref.dtype)


def _mix_b(za, prm, l, tt):
    m = za.shape[0]
    return pl.pallas_call(
        _mix_b_kernel,
        grid=(m // tt,),
        in_specs=[
            pl.BlockSpec((tt, W_B), lambda i: (i, 2)),
            pl.BlockSpec((tt, W_B), lambda i: (i, 3)),
            pl.BlockSpec((None, G_B, CHUNK, CHUNK), lambda i: (l, 0, 0, 0)),
            pl.BlockSpec((None, CHUNK, W_B), lambda i: (l, 0, 0)),
            pl.BlockSpec((None, 1, W_B), lambda i: (l, 0, 0)),
            pl.BlockSpec((None, 1, W_B), lambda i: (l, 0, 0)),
        ],
        out_specs=pl.BlockSpec((tt, W_B), lambda i: (i, 0)),
        out_shape=jax.ShapeDtypeStruct((m, W_B), BF16),
        compiler_params=_cparams(("arbitrary",)),
        name="mix_b",
    )(za, za, prm["w_s"], prm["bs_rows"], prm["vn_g"], prm["vn_b"])


def _attn_kernel(q_ref, k_ref, v_ref, lq1_ref, lk1_ref, lq2_ref, lk2_ref, g_ref, o_ref,
                 m_ref, l_ref, acc_ref, *, lambda_init, tk):
    qi = pl.program_id(2)
    tq = q_ref.shape[0]
    m_ref[...] = jnp.full(m_ref.shape, NEG_INF, F32)
    l_ref[...] = jnp.zeros(l_ref.shape, F32)
    acc_ref[...] = jnp.zeros(acc_ref.shape, F32)
    q = q_ref[...]

    def block(kstart, masked):
        k = k_ref[pl.ds(kstart, tk), :]
        v = v_ref[pl.ds(kstart, tk), :]
        if masked:
            row = lax.broadcasted_iota(jnp.int32, (tq, tk), 0) + qi * tq
            col = lax.broadcasted_iota(jnp.int32, (tq, tk), 1) + kstart
            visible = col <= row
        for c in range(2):
            s = lax.dot_general(q[:, c * HD_C:(c + 1) * HD_C], k[:, c * HD_C:(c + 1) * HD_C],
                                (((1,), (1,)), ((), ())), preferred_element_type=F32)
            if masked:
                s = jnp.where(visible, s, NEG_INF)
            m_prev = m_ref[c]
            m_new = jnp.maximum(m_prev, jnp.max(s, axis=-1, keepdims=True))
            alpha = jnp.exp2(m_prev - m_new)
            p = jnp.exp2(s - _lane_tile(m_new, tk))
            l_ref[c] = alpha * l_ref[c] + jnp.sum(p, axis=-1, keepdims=True)
            acc_ref[c] = _lane_tile(alpha, DV_C) * acc_ref[c] + jnp.dot(
                p.astype(BF16), v, preferred_element_type=F32)
            m_ref[c] = m_new

    n_full = (qi * tq) // tk

    def body(i, carry):
        block(pl.multiple_of(i * tk, tk), False)
        return carry

    lax.fori_loop(0, n_full, body, 0)
    block(pl.multiple_of(n_full * tk, tk), True)

    lam = _diff_lambda(lq1_ref[...], lk1_ref[...], lq2_ref[...], lk2_ref[...], lambda_init)
    o = acc_ref[0] / _lane_tile(l_ref[0], DV_C) - lam * (acc_ref[1] / _lane_tile(l_ref[1], DV_C))
    ms = jnp.mean(o * o, axis=-1, keepdims=True)
    o_ref[...] = (o * lax.rsqrt(ms + LN_EPS) * g_ref[...] * (1.0 - lambda_init)).astype(o_ref.dtype)


def _attn(qkv, prm, l, nb, seq, tq, tk, lambda_init):
    m = qkv.shape[0]
    nq = seq // tq
    lamspec = pl.BlockSpec((None, 1, HD_C), lambda b, h, qi: (l, 0, 0))
    return pl.pallas_call(
        functools.partial(_attn_kernel, lambda_init=lambda_init, tk=tk),
        grid=(nb, H_C, nq),
        in_specs=[
            pl.BlockSpec((tq, DV_C), lambda b, h, qi: (b * nq + qi, h)),
            pl.BlockSpec((seq, DV_C), lambda b, h, qi: (b, H_C + h)),
            pl.BlockSpec((seq, DV_C), lambda b, h, qi: (b, 2 * H_C + h)),
            lamspec, lamspec, lamspec, lamspec,
            pl.BlockSpec((None, 1, DV_C), lambda b, h, qi: (l, 0, 0)),
        ],
        out_specs=pl.BlockSpec((tq, DV_C), lambda b, h, qi: (b * nq + qi, h)),
        out_shape=jax.ShapeDtypeStruct((m, W_C), BF16),
        scratch_shapes=[
            pltpu.VMEM((2, tq, LANES), F32),
            pltpu.VMEM((2, tq, LANES), F32),
            pltpu.VMEM((2, tq, DV_C), F32),
        ],
        compiler_params=_cparams(("arbitrary",) * 3),
        name="attn",
    )(qkv, qkv, qkv, prm["lam_q1"], prm["lam_k1"], prm["lam_q2"], prm["lam_k2"], prm["subln_g"])


def _outproj_kernel(ya_ref, yb_ref, o_ref, x_ref, w_ref, g_ref, b_ref, out_ref, outb_ref, *, alpha):
    cat = jnp.concatenate([ya_ref[...].astype(BF16), yb_ref[...].astype(BF16), o_ref[...].astype(BF16)],
                          axis=-1)
    y = jnp.dot(cat, w_ref[...], preferred_element_type=F32)
    r = _layer_norm(alpha * x_ref[...] + y, g_ref[...], b_ref[...])
    out_ref[...] = r
    outb_ref[...] = r.astype(BF16)


def _outproj(ya, yb, o, x, prm, l, tm, alpha):
    m = x.shape[0]
    vec = pl.BlockSpec((None, 1, D_MODEL), lambda i: (l, 0, 0))
    row = pl.BlockSpec((tm, D_MODEL), lambda i: (i, 0))
    return pl.pallas_call(
        functools.partial(_outproj_kernel, alpha=alpha),
        grid=(m // tm,),
        in_specs=[
            pl.BlockSpec((tm, W_A), lambda i: (i, 0)),
            pl.BlockSpec((tm, W_B), lambda i: (i, 0)),
            pl.BlockSpec((tm, W_C), lambda i: (i, 0)),
            row,
            pl.BlockSpec((None, D_MODEL, D_MODEL), lambda i: (l, 0, 0)),
            vec, vec,
        ],
        out_specs=[row, row],
        out_shape=[jax.ShapeDtypeStruct((m, D_MODEL), F32), jax.ShapeDtypeStruct((m, D_MODEL), BF16)],
        compiler_params=_cparams(("arbitrary",)),
        name="outproj",
    )(ya, yb, o, x, prm["w_out"], prm["ln1_g"], prm["ln1_b"])


def _swiglu_kernel(x_ref, xb_ref, wg_ref, wu_ref, wd_ref, g_ref, b_ref, out_ref, outb_ref, acc_ref, *, alpha):
    f = pl.program_id(1)

    @pl.when(f == 0)
    def _():
        acc_ref[...] = jnp.zeros(acc_ref.shape, F32)

    xb = xb_ref[...]
    hg = jnp.dot(xb, wg_ref[...], preferred_element_type=F32)
    hu = jnp.dot(xb, wu_ref[...], preferred_element_type=F32)
    h = (hg * jax.nn.sigmoid(hg)) * hu
    acc_ref[...] += jnp.dot(h.astype(BF16), wd_ref[...], preferred_element_type=F32)

    @pl.when(f == pl.num_programs(1) - 1)
    def _():
        r = _layer_norm(alpha * x_ref[...] + acc_ref[...], g_ref[...], b_ref[...])
        out_ref[...] = r
        outb_ref[...] = r.astype(BF16)


def _swiglu(x, xb, prm, l, tm, tf, alpha):
    m = x.shape[0]
    vec = pl.BlockSpec((None, 1, D_MODEL), lambda i, f: (l, 0, 0))
    row = pl.BlockSpec((tm, D_MODEL), lambda i, f: (i, 0))
    return pl.pallas_call(
        functools.partial(_swiglu_kernel, alpha=alpha),
        grid=(m // tm, D_FF // tf),
        in_specs=[
            row, row,
            pl.BlockSpec((None, D_MODEL, tf), lambda i, f: (l, 0, f)),
            pl.BlockSpec((None, D_MODEL, tf), lambda i, f: (l, 0, f)),
            pl.BlockSpec((None, tf, D_MODEL), lambda i, f: (l, f, 0)),
            vec, vec,
        ],
        out_specs=[row, row],
        out_shape=[jax.ShapeDtypeStruct((m, D_MODEL), F32), jax.ShapeDtypeStruct((m, D_MODEL), BF16)],
        scratch_shapes=[pltpu.VMEM((tm, D_MODEL), F32)],
        compiler_params=_cparams(("arbitrary", "arbitrary")),
        name="swiglu",
    )(x, xb, prm["w_gate"], prm["w_up"], prm["w_down"], prm["ln2_g"], prm["ln2_b"])


def _sample_mix_kernel(za_ref, cbuf_ref, h0_ref, cw_ref, cb_ref, wa_ref, ba_ref, wx_ref, bx_ref, lam_ref,
                       vng_ref, vnb_ref, w4_ref, b4_ref,
                       ya_ref, yb_ref, hl_ref, cout_ref, vn_ref):
    nt = vn_ref.shape[0]
    nb = h0_ref.shape[0]
    zc = 4 * MIX_W // LANES
    gc = MIX_W // LANES

    def rows(t, grp):
        return jnp.concatenate(
            [za_ref[pl.ds(t * zc + grp * gc + c, nb, stride=nt * zc), :] for c in range(gc)], axis=-1)

    def put_rows(ref, t, val):
        for c in range(gc):
            ref[pl.ds(t * gc + c, nb, stride=nt * gc), :] = val[:, c * LANES:(c + 1) * LANES]

    xa = [rows(t, 0) for t in range(nt)]
    xp = [cbuf_ref[j] for j in range(CONV_W - 1)] + xa
    cw = cw_ref[...]
    xc = []
    for t in range(nt):
        acc = cb_ref[...] + cw[0:1] * xp[t]
        for j in range(1, CONV_W):
            acc = acc + cw[j:j + 1] * xp[t + j]
        xc.append(acc)
    a, b = _rglru_coeffs(jnp.concatenate(xc, axis=0), wa_ref, ba_ref, wx_ref, bx_ref, lam_ref)
    h = h0_ref[...]
    for t in range(nt):
        h = a[t * nb:(t + 1) * nb] * h + b[t * nb:(t + 1) * nb]
        put_rows(ya_ref, t, h * jax.nn.gelu(rows(t, 1)))
    hl_ref[...] = h
    for j in range(CONV_W - 1):
        cout_ref[j] = xp[nt + j]

    vn = []
    for t in range(nt):
        v = _layer_norm(jax.nn.gelu(rows(t, 3)), vng_ref[...], vnb_ref[...])
        vn_ref[t] = v
        vn.append(v)
    for t in range(nt):
        mix = b4_ref[t:t + 1]
        for s in range(t + 1):
            mix = mix + w4_ref[t, s:s + 1] * vn[s]
        put_rows(yb_ref, t, jax.nn.gelu(rows(t, 2)) * mix)


def _sample_mix(za, cbuf, h0, prm, l, nb, nt):
    m = nb * nt
    full = lambda shape: pl.BlockSpec(shape, lambda i: (0,) * len(shape))
    lsel = lambda shape: pl.BlockSpec((None,) + shape, lambda i: (l,) + (0,) * len(shape))
    return pl.pallas_call(
        _sample_mix_kernel,
        grid=(1,),
        in_specs=[
            full((m * 4 * MIX_W // LANES, LANES)),
            lsel((CONV_W - 1, nb, W_A)),
            lsel((nb, W_A)),
            lsel((CONV_W, W_A)), lsel((1, W_A)),
            lsel((H_A, BW_A, BW_A)), lsel((1, W_A)),
            lsel((H_A, BW_A, BW_A)), lsel((1, W_A)),
            lsel((1, W_A)),
            lsel((1, W_B)), lsel((1, W_B)),
            lsel((nt, nt, W_B)), lsel((nt, W_B)),
        ],
        out_specs=[
            full((m * W_A // LANES, LANES)), full((m * W_B // LANES, LANES)), full((nb, W_A)),
            full((CONV_W - 1, nb, W_A)), full((nt, nb, W_B)),
        ],
        out_shape=[
            jax.ShapeDtypeStruct((m * W_A // LANES, LANES), F32),
            jax.ShapeDtypeStruct((m * W_B // LANES, LANES), F32),
            jax.ShapeDtypeStruct((nb, W_A), F32),
            jax.ShapeDtypeStruct((CONV_W - 1, nb, W_A), F32),
            jax.ShapeDtypeStruct((nt, nb, W_B), F32),
        ],
        compiler_params=_cparams(("arbitrary",)),
        name="sample_mix",
    )(za, cbuf, h0, prm["conv_w"], prm["conv_b"], prm["w_a"], prm["b_a"], prm["w_x"], prm["b_x"],
      prm["lru_lam"], prm["vn_g"], prm["vn_b"], prm["w4"], prm["b4"])


def _expand_queries(q):
    hc = lax.broadcasted_iota(jnp.int32, (2 * H_C, W_C), 0)
    blk = lax.broadcasted_iota(jnp.int32, (2 * H_C, W_C), 1) // HD_C
    keep = hc == blk
    return jnp.concatenate(
        [jnp.where(keep, jnp.broadcast_to(q[t:t + 1], (2 * H_C, W_C)), 0.0) for t in range(q.shape[0])],
        axis=0)


def _page_keys(ref, page):
    return jnp.concatenate(
        [ref[pl.ds(_k_piece(hc), page, stride=HEAD_ROWS), :] for hc in range(2 * H_C)], axis=-1)


def _page_values(ref, page):
    return jnp.concatenate(
        [ref[pl.ds(_v_piece(h, half), page, stride=HEAD_ROWS), :] for h in range(H_C) for half in range(2)],
        axis=-1)


def _paged_attn_kernel(pt_ref, q_ref, kn_ref, vn_ref, lq1_ref, lk1_ref, lq2_ref, lk2_ref, g_ref, *rest,
                       pages_per_step, lambda_init):
    del pt_ref
    k_refs = rest[:pages_per_step]
    v_refs = rest[pages_per_step:2 * pages_per_step]
    o_ref, qe_ref, m_ref, l_ref, acc_ref = rest[2 * pages_per_step:]
    j = pl.program_id(1)
    nt = q_ref.shape[0]
    page = k_refs[0].shape[0] // HEAD_ROWS

    @pl.when(j == 0)
    def _():
        qe_ref[...] = _expand_queries(q_ref[...]).astype(BF16)
        m_ref[...] = jnp.full(m_ref.shape, NEG_INF, F32)
        l_ref[...] = jnp.zeros(l_ref.shape, F32)
        acc_ref[...] = jnp.zeros(acc_ref.shape, F32)

    qe = qe_ref[...]
    s = jnp.concatenate(
        [lax.dot_general(qe, _page_keys(k_refs[p], page).astype(BF16), (((1,), (1,)), ((), ())),
                         preferred_element_type=F32) for p in range(pages_per_step)], axis=-1)
    m_prev = m_ref[...]
    m_new = jnp.maximum(m_prev, jnp.max(s, axis=-1, keepdims=True))
    alpha = jnp.exp2(m_prev - m_new)
    p = jnp.exp2(s - m_new)
    l_ref[...] = alpha * l_ref[...] + jnp.sum(p, axis=-1, keepdims=True)
    pb = p.astype(BF16)
    pv = jnp.dot(pb[:, 0:page], _page_values(v_refs[0], page).astype(BF16), preferred_element_type=F32)
    for i in range(1, pages_per_step):
        pv = pv + jnp.dot(pb[:, i * page:(i + 1) * page], _page_values(v_refs[i], page).astype(BF16),
                          preferred_element_type=F32)
    acc_ref[...] = alpha * acc_ref[...] + pv
    m_ref[...] = m_new

    @pl.when(j == pl.num_programs(1) - 1)
    def _():
        qe32 = _expand_queries(q_ref[...])
        trow = lax.broadcasted_iota(jnp.int32, (nt * 2 * H_C, 1), 0) // (2 * H_C)
        kn = kn_ref[...]
        vn = vn_ref[...]
        sc = [jnp.where(trow >= t, jnp.sum(qe32 * kn[t:t + 1], axis=-1, keepdims=True), NEG_INF)
              for t in range(nt)]
        m_prev = m_ref[...]
        m_new = m_prev
        for t in range(nt):
            m_new = jnp.maximum(m_new, sc[t])
        alpha = jnp.exp2(m_prev - m_new)
        lsum = alpha * l_ref[...]
        acc = alpha * acc_ref[...]
        for t in range(nt):
            pt = jnp.exp2(sc[t] - m_new)
            lsum = lsum + pt
            acc = acc + pt * vn[t:t + 1]
        accn = acc / lsum
        lam = _diff_lambda(lq1_ref[...], lk1_ref[...], lq2_ref[...], lk2_ref[...], lambda_init)
        hc = lax.broadcasted_iota(jnp.int32, (nt * 2 * H_C, 1), 0) % (2 * H_C)
        outs = []
        for h in range(H_C):
            w = jnp.where(hc == 2 * h, 1.0, jnp.where(hc == 2 * h + 1, -lam, 0.0))
            o = jnp.sum((accn[:, h * DV_C:(h + 1) * DV_C] * w).reshape(nt, 2 * H_C, DV_C), axis=1)
            ms = jnp.mean(o * o, axis=-1, keepdims=True)
            outs.append(o * lax.rsqrt(ms + LN_EPS) * g_ref[...] * (1.0 - lambda_init))
        o_ref[...] = jnp.concatenate(outs, axis=-1)


def _paged_attn(page_table, qkv3, cache_k, cache_v, prm, l, pages_per_step, lambda_init):
    nb, nt, _ = qkv3.shape
    n_pages = page_table.shape[0] // nb
    page_rows = cache_k.shape[2]
    lamspec = pl.BlockSpec((None, 1, HD_C), lambda b, j, pt: (l, 0, 0))

    def page_spec(p):
        return pl.BlockSpec((None, None, page_rows, LANES),
                            lambda b, j, pt: (l, pt[b * n_pages + j * pages_per_step + p], 0, 0))

    grid_spec = pltpu.PrefetchScalarGridSpec(
        num_scalar_prefetch=1,
        grid=(nb, n_pages // pages_per_step),
        in_specs=[
            pl.BlockSpec((None, nt, W_C), lambda b, j, pt: (b, 0, 0)),
            pl.BlockSpec((None, nt, W_C), lambda b, j, pt: (b, 0, 1)),
            pl.BlockSpec((None, nt, W_C), lambda b, j, pt: (b, 0, 2)),
            lamspec, lamspec, lamspec, lamspec,
            pl.BlockSpec((None, 1, DV_C), lambda b, j, pt: (l, 0, 0)),
        ] + [page_spec(p) for p in range(pages_per_step)] * 2,
        out_specs=pl.BlockSpec((None, nt, W_C), lambda b, j, pt: (b, 0, 0)),
        scratch_shapes=[
            pltpu.VMEM((nt * 2 * H_C, W_C), BF16),
            pltpu.VMEM((nt * 2 * H_C, 1), F32),
            pltpu.VMEM((nt * 2 * H_C, 1), F32),
            pltpu.VMEM((nt * 2 * H_C, W_C), F32),
        ],
    )
    return pl.pallas_call(
        functools.partial(_paged_attn_kernel, pages_per_step=pages_per_step, lambda_init=lambda_init),
        grid_spec=grid_spec,
        out_shape=jax.ShapeDtypeStruct((nb, nt, W_C), F32),
        compiler_params=_cparams(("arbitrary", "arbitrary")),
        name="paged_attn",
    )(page_table, qkv3, qkv3, qkv3, prm["lam_q1"], prm["lam_k1"], prm["lam_q2"], prm["lam_k2"],
      prm["subln_g"], *([cache_k] * pages_per_step), *([cache_v] * pages_per_step))


def _values_to_head_rows(v):
    lead = v.shape[:-3]
    tokens = v.shape[-3]
    v = v.reshape(lead + (tokens, H_C, 2, LANES))
    v = jnp.swapaxes(v, -3, -2)
    return v.reshape(lead + (tokens * HEAD_ROWS, LANES))


def _head_rows_to_values(r, lead):
    depth = r.shape[0]
    v = r.reshape((depth,) + lead + (2, H_C, LANES))
    v = jnp.swapaxes(v, -3, -2)
    return v.reshape((depth,) + lead + (H_C, DV_C))


def _largest_tile(n, cap, mult):
    t = min(n, cap)
    while n % t or t % mult:
        t -= 1
    return t


def kernel(x_prompt, x_sample, cache_k, cache_v, state_lru_h, state_conv, page_table, w_in, w_out, conv_w,
           conv_b, w_a, b_a, w_x, b_x, lru_lam, vn_g, vn_b, w_s, b_s, lam_q1, lam_k1, lam_q2, lam_k2,
           subln_g, ln1_g, ln1_b, ln2_g, ln2_b, w_gate, w_up, w_down):
    depth = w_in.shape[0]
    nbp, seq, _ = x_prompt.shape
    nbs, nts, _ = x_sample.shape
    n_pool, page = cache_k.shape[1], cache_k.shape[2]
    n_pages = page_table.shape[1]
    alpha = (2.0 * depth) ** 0.25
    mp, ms = nbp * seq, nbs * nts

    row3 = lambda a: a.reshape(depth, 1, a.shape[-1])
    prm = {
        "w_in": w_in.astype(BF16), "w_out": w_out.astype(BF16),
        "w_gate": w_gate.astype(BF16), "w_up": w_up.astype(BF16), "w_down": w_down.astype(BF16),
        "w_a": w_a.astype(BF16), "w_x": w_x.astype(BF16),
        "conv_w": conv_w, "conv_b": row3(conv_b), "b_a": row3(b_a), "b_x": row3(b_x),
        "lru_lam": row3(lru_lam), "vn_g": row3(vn_g), "vn_b": row3(vn_b), "w_s": w_s,
        "bs_rows": jnp.repeat(jnp.transpose(b_s, (0, 2, 1)), GW_B, axis=-1),
        "w4": jnp.repeat(jnp.transpose(w_s[:, :, :nts, :nts], (0, 2, 3, 1)), GW_B, axis=-1),
        "b4": jnp.repeat(jnp.transpose(b_s[:, :, :nts], (0, 2, 1)), GW_B, axis=-1),
        "lam_q1": row3(lam_q1), "lam_k1": row3(lam_k1), "lam_q2": row3(lam_q2), "lam_k2": row3(lam_k2),
        "subln_g": row3(subln_g), "ln1_g": row3(ln1_g), "ln1_b": row3(ln1_b),
        "ln2_g": row3(ln2_g), "ln2_b": row3(ln2_b),
    }
    ck = cache_k.reshape(depth, n_pool, page * HEAD_ROWS, LANES)
    cv = _values_to_head_rows(cache_v)
    pt_flat = page_table.reshape(-1)
    sconv = jnp.transpose(state_conv, (0, 2, 1, 3))
    zero_cbuf = jnp.zeros((nbp, SUBLANES, W_A), F32)
    zero_h = jnp.zeros((nbp, 1, W_A), F32)

    tm_in = _largest_tile(mp, 1024, CHUNK)
    tm_p = _largest_tile(mp, 512, CHUNK)
    tt_a = _largest_tile(seq, 256, SUBLANES)
    tk = _largest_tile(seq, 512, CHUNK)
    tq = _largest_tile(tk, 256, CHUNK)
    tf = _largest_tile(D_FF, 512, 128)
    pps = _largest_tile(n_pages, 8, 1)

    xp = x_prompt.reshape(mp, D_MODEL)
    xs = x_sample.reshape(ms, D_MODEL)
    xp_b, xs_b = xp.astype(BF16), xs.astype(BF16)
    kbuf_p = jnp.zeros((depth, mp * HEAD_ROWS, LANES), F32)
    vbuf_p = jnp.zeros((depth, mp * HEAD_ROWS, LANES), F32)
    kbuf_s = jnp.zeros((depth, ms * HEAD_ROWS, LANES), F32)
    vbuf_s = jnp.zeros((depth, ms * HEAD_ROWS, LANES), F32)
    outs = [[] for _ in range(5)]
    for l in range(depth):
        lambda_init = 0.8 - 0.6 * math.exp(-0.3 * l)
        za, qkv, kbuf_p, vbuf_p = _inproj(xp_b, prm["w_in"], kbuf_p, vbuf_p, l, tm_in, BF16)
        ya, h_p, c_p = _mix_a(za, zero_cbuf, zero_h, prm, l, nbp, tt_a)
        yb = _mix_b(za, prm, l, tm_p)
        o = _attn(qkv, prm, l, nbp, seq, tq, tk, lambda_init)
        xp, xp_b = _outproj(ya, yb, o, xp, prm, l, tm_p, alpha)
        xp, xp_b = _swiglu(xp, xp_b, prm, l, tm_p, tf, alpha)
        za_s, qkv_s, kbuf_s, vbuf_s = _inproj(xs_b, prm["w_in"], kbuf_s, vbuf_s, l, ms, F32)
        ya_s, yb_s, h_s, c_s, vn_s = _sample_mix(za_s.reshape(-1, LANES), sconv, state_lru_h, prm, l, nbs,
                                                  nts)
        ya_s, yb_s = ya_s.reshape(ms, W_A), yb_s.reshape(ms, W_B)
        o_s = _paged_attn(pt_flat, qkv_s.reshape(nbs, nts, 3 * W_C), ck, cv, prm, l, pps, lambda_init)
        xs, xs_b = _outproj(ya_s, yb_s, o_s.reshape(ms, W_C), xs, prm, l, ms, alpha)
        xs, xs_b = _swiglu(xs, xs_b, prm, l, ms, tf, alpha)

        for lst, val in zip(outs, (h_p, h_s, c_p, c_s, vn_s)):
            lst.append(val)

    hp, hs, cp, cs, chv = [jnp.stack(v) for v in outs]
    return (
        xp.reshape(nbp, seq, D_MODEL),
        xs.reshape(nbs, nts, D_MODEL),
        kbuf_p.reshape(depth, nbp, seq, H_C, 2, HD_C),
        _head_rows_to_values(vbuf_p, (nbp, seq)),
        kbuf_s.reshape(depth, nbs, nts, H_C, 2, HD_C),
        _head_rows_to_values(vbuf_s, (nbs, nts)),
        hp.reshape(depth, nbp, W_A),
        hs,
        cp[:, :, SUBLANES - (CONV_W - 1):, :],
        jnp.transpose(cs, (0, 2, 1, 3)),
        jnp.transpose(chv, (0, 2, 1, 3)),
    )
```

```python
import functools
import math

import jax
import jax.numpy as jnp
from jax import lax
from jax.experimental import pallas as pl
from jax.experimental.pallas import tpu as pltpu

F32 = jnp.float32
BF16 = jnp.bfloat16

D_MODEL = 2048
W_A = 512
H_A = 4
BW_A = 128
CONV_W = 4
RG_C = 8.0
W_B = 512
G_B = 4
GW_B = 128
CHUNK = 128
W_C = 1024
H_C = 4
HD_C = 128
DV_C = 256
W_IN = 2 * W_A + 2 * W_B + 3 * W_C
D_FF = 5632
LN_EPS = 1e-5
NEG_INF = -1e30
QK_SCALE = HD_C ** -0.5 * math.log2(math.e)

SUBLANES = 8
LANES = 128
COL_TILE = 1024
MIX_W = 512
HEAD_ROWS = W_C // LANES
VMEM_LIMIT = 56 * 1024 * 1024


def _cparams(sem):
    return pltpu.CompilerParams(dimension_semantics=sem, vmem_limit_bytes=VMEM_LIMIT)


def _layer_norm(r, g, b):
    mu = jnp.mean(r, axis=-1, keepdims=True)
    d = r - mu
    var = jnp.mean(d * d, axis=-1, keepdims=True)
    return d * lax.rsqrt(var + LN_EPS) * g + b


def _softplus(x):
    return jnp.maximum(x, 0.0) + jnp.log1p(jnp.exp(-jnp.abs(x)))


def _neg_expm1_nonpos(x):
    t = jnp.tanh(0.5 * x)
    return -2.0 * t / (1.0 - t)


def _diff_lambda(lq1, lk1, lq2, lk2, lambda_init):
    s1 = jnp.sum(lq1 * lk1, axis=-1, keepdims=True)
    s2 = jnp.sum(lq2 * lk2, axis=-1, keepdims=True)
    return jnp.exp(s1) - jnp.exp(s2) + lambda_init


def _lane_tile(x, width):
    return jnp.concatenate([x] * (width // LANES), axis=-1)


def _k_piece(hc):
    return hc


def _v_piece(h, half):
    return half * H_C + h


def _inproj_kernel(x_ref, w_ref, kin_ref, vin_ref, za_ref, qkv_ref, kf_ref, vf_ref, *wb_refs):
    del kin_ref, vin_ref
    j = pl.program_id(1)
    tm = x_ref.shape[0]
    w = w_ref[...].astype(BF16)
    for wb_ref in wb_refs:
        wb_ref[...] = w
    z = jnp.dot(x_ref[...], w, preferred_element_type=F32)

    @pl.when(j < 2)
    def _():
        za_ref[...] = z

    @pl.when(j == 2)
    def _():
        qkv_ref[...] = (z * QK_SCALE).astype(qkv_ref.dtype)

    @pl.when(j >= 3)
    def _():
        qkv_ref[...] = z.astype(qkv_ref.dtype)

    @pl.when(j == 3)
    def _():
        for hc in range(2 * H_C):
            kf_ref[pl.ds(_k_piece(hc), tm, stride=HEAD_ROWS), :] = z[:, hc * HD_C:(hc + 1) * HD_C]

    @pl.when(j == 4)
    def _():
        for h in range(H_C):
            for half in range(2):
                c0 = h * DV_C + half * LANES
                vf_ref[pl.ds(_v_piece(h, half), tm, stride=HEAD_ROWS), :] = z[:, c0:c0 + LANES]


def _inproj(xb, w, wl, kbuf, vbuf, l, tm, qkv_dtype, emit_weights):
    m = xb.shape[0]
    nj = W_IN // COL_TILE
    kv_spec = pl.BlockSpec((None, tm * HEAD_ROWS, LANES), lambda i, j: (l, i, 0))
    out_specs = [
        pl.BlockSpec((tm, COL_TILE), lambda i, j: (i, jnp.minimum(j, 1))),
        pl.BlockSpec((tm, COL_TILE), lambda i, j: (i, jnp.clip(j - 2, 0, 2))),
        kv_spec, kv_spec,
    ]
    out_shape = [
        jax.ShapeDtypeStruct((m, 4 * MIX_W), F32),
        jax.ShapeDtypeStruct((m, 3 * W_C), qkv_dtype),
        jax.ShapeDtypeStruct(kbuf.shape, F32),
        jax.ShapeDtypeStruct(vbuf.shape, F32),
    ]
    if emit_weights:
        out_specs.append(pl.BlockSpec((None, D_MODEL, COL_TILE), lambda i, j: (0, 0, j)))
        out_shape.append(jax.ShapeDtypeStruct((1, D_MODEL, W_IN), BF16))
    return pl.pallas_call(
        _inproj_kernel,
        grid=(m // tm, nj),
        in_specs=[
            pl.BlockSpec((tm, D_MODEL), lambda i, j: (i, 0)),
            pl.BlockSpec((None, D_MODEL, COL_TILE), lambda i, j: (wl, 0, j)),
            pl.BlockSpec(memory_space=pl.ANY),
            pl.BlockSpec(memory_space=pl.ANY),
        ],
        out_specs=out_specs,
        out_shape=out_shape,
        input_output_aliases={2: 2, 3: 3},
        compiler_params=_cparams(("arbitrary", "arbitrary")),
        name="inproj",
    )(xb, w, kbuf, vbuf)


def _shift_rows(x, s, fill):
    n, w = x.shape
    if s % SUBLANES == 0:
        return jnp.concatenate([jnp.full((s, w), fill, x.dtype), x[:n - s]], axis=0)
    row = lax.broadcasted_iota(jnp.int32, x.shape, 0)
    return jnp.where(row < s, fill, pltpu.roll(x, s, 0))


def _block_gate(xcb, w_ref, b_ref):
    parts = [jnp.dot(xcb[:, h * BW_A:(h + 1) * BW_A], w_ref[h], preferred_element_type=F32)
             for h in range(H_A)]
    return jax.nn.sigmoid(jnp.concatenate(parts, axis=-1) + b_ref[...])


def _rglru_coeffs(xc, wa_ref, ba_ref, wx_ref, bx_ref, lam_ref):
    xcb = xc.astype(BF16)
    r = _block_gate(xcb, wa_ref, ba_ref)
    gi = _block_gate(xcb, wx_ref, bx_ref)
    log_a = -RG_C * r * _softplus(-lam_ref[...])
    a = jnp.exp(log_a)
    b = jnp.sqrt(_neg_expm1_nonpos(2.0 * log_a)) * (gi * xc)
    return a, b


def _mix_a_kernel(xa_ref, ga_ref, cbuf_ref, h0_ref, cw_ref, cb_ref, wa_ref, ba_ref, wx_ref, bx_ref,
                  lam_ref, ya_ref, hl_ref, cout_ref, tail_ref, hc_ref):
    t = pl.program_id(1)
    tt = xa_ref.shape[0]

    @pl.when(t == 0)
    def _():
        tail_ref[...] = cbuf_ref[...]
        hc_ref[...] = h0_ref[...]

    xa = xa_ref[...]
    tail = tail_ref[...]
    row8 = lax.broadcasted_iota(jnp.int32, tail.shape, 0)

    def delayed(k):
        r = pltpu.roll(xa, k, 0)
        first = jnp.where(row8 < k, pltpu.roll(tail, k, 0), r[0:SUBLANES])
        return jnp.concatenate([first, r[SUBLANES:]], axis=0)

    cw = cw_ref[...]
    xc = cb_ref[...] + cw[3:4] * xa
    for k in range(1, CONV_W):
        xc = xc + cw[CONV_W - 1 - k:CONV_W - k] * delayed(k)
    tail_ref[...] = xa[tt - SUBLANES:tt]

    a, b = _rglru_coeffs(xc, wa_ref, ba_ref, wx_ref, bx_ref, lam_ref)
    s = 1
    while s < tt:
        b = b + a * _shift_rows(b, s, 0.0)
        a = a * _shift_rows(a, s, 1.0)
        s *= 2
    h = a * hc_ref[...] + b
    hc_ref[...] = h[tt - 1:tt]
    ya_ref[...] = (h * jax.nn.gelu(ga_ref[...])).astype(ya_ref.dtype)

    @pl.when(t == pl.num_programs(1) - 1)
    def _():
        hl_ref[...] = h[tt - 1:tt]
        cout_ref[...] = xa[tt - SUBLANES:tt]


def _mix_a(za, cbuf8, h0, prm, l, nb, tt):
    m = za.shape[0]
    nt = m // nb // tt
    vec = pl.BlockSpec((None, 1, W_A), lambda b, t: (l, 0, 0))
    blk = pl.BlockSpec((None, H_A, BW_A, BW_A), lambda b, t: (l, 0, 0, 0))
    return pl.pallas_call(
        _mix_a_kernel,
        grid=(nb, nt),
        in_specs=[
            pl.BlockSpec((tt, W_A), lambda b, t: (b * nt + t, 0)),
            pl.BlockSpec((tt, W_A), lambda b, t: (b * nt + t, 1)),
            pl.BlockSpec((None, SUBLANES, W_A), lambda b, t: (b, 0, 0)),
            pl.BlockSpec((None, 1, W_A), lambda b, t: (b, 0, 0)),
            pl.BlockSpec((None, CONV_W, W_A), lambda b, t: (l, 0, 0)),
            vec, blk, vec, blk, vec, vec,
        ],
        out_specs=[
            pl.BlockSpec((tt, W_A), lambda b, t: (b * nt + t, 0)),
            pl.BlockSpec((None, 1, W_A), lambda b, t: (b, 0, 0)),
            pl.BlockSpec((None, SUBLANES, W_A), lambda b, t: (b, 0, 0)),
        ],
        out_shape=[
            jax.ShapeDtypeStruct((m, W_A), BF16),
            jax.ShapeDtypeStruct((nb, 1, W_A), F32),
            jax.ShapeDtypeStruct((nb, SUBLANES, W_A), F32),
        ],
        scratch_shapes=[pltpu.VMEM((SUBLANES, W_A), F32), pltpu.VMEM((1, W_A), F32)],
        compiler_params=_cparams(("arbitrary", "arbitrary")),
        name="mix_a",
    )(za, za, cbuf8, h0, prm["conv_w"], prm["conv_b"], prm["w_a"], prm["b_a"], prm["w_x"], prm["b_x"],
      prm["lru_lam"])


def _mix_b_kernel(ub_ref, vb_ref, ws_ref, bs_ref, g_ref, b_ref, yb_ref):
    tt = ub_ref.shape[0]
    vn = _layer_norm(jax.nn.gelu(vb_ref[...]), g_ref[...], b_ref[...]).astype(BF16)
    row = lax.broadcasted_iota(jnp.int32, (CHUNK, CHUNK), 0)
    col = lax.broadcasted_iota(jnp.int32, (CHUNK, CHUNK), 1)
    wt = [jnp.where(col <= row, ws_ref[g], 0.0).astype(BF16) for g in range(G_B)]
    bias = bs_ref[...]
    chunks = []
    for c in range(tt // CHUNK):
        vc = vn[c * CHUNK:(c + 1) * CHUNK]
        parts = [jnp.dot(wt[g], vc[:, g * GW_B:(g + 1) * GW_B], preferred_element_type=F32)
                 for g in range(G_B)]
        chunks.append(jnp.concatenate(parts, axis=-1) + bias)
    mix = jnp.concatenate(chunks, axis=0)
    yb_ref[...] = (jax.nn.gelu(ub_ref[...]) * mix).astype(yb_ref.dtype)


def _mix_b(za, prm, l, tt):
    m = za.shape[0]
    return pl.pallas_call(
        _mix_b_kernel,
        grid=(m // tt,),
        in_specs=[
            pl.BlockSpec((tt, W_B), lambda i: (i, 2)),
            pl.BlockSpec((tt, W_B), lambda i: (i, 3)),
            pl.BlockSpec((None, G_B, CHUNK, CHUNK), lambda i: (l, 0, 0, 0)),
            pl.BlockSpec((None, CHUNK, W_B), lambda i: (l, 0, 0)),
            pl.BlockSpec((None, 1, W_B), lambda i: (l, 0, 0)),
            pl.BlockSpec((None, 1, W_B), lambda i: (l, 0, 0)),
        ],
        out_specs=pl.BlockSpec((tt, W_B), lambda i: (i, 0)),
        out_shape=jax.ShapeDtypeStruct((m, W_B), BF16),
        compiler_params=_cparams(("arbitrary",)),
        name="mix_b",
    )(za, za, prm["w_s"], prm["bs_rows"], prm["vn_g"], prm["vn_b"])


def _attn_kernel(q_ref, k_ref, v_ref, lq1_ref, lk1_ref, lq2_ref, lk2_ref, g_ref, o_ref,
                 m_ref, l_ref, acc_ref, p_ref, a_ref, *, lambda_init, tk):
    qi = pl.program_id(2)
    tq = q_ref.shape[0]
    m_ref[...] = jnp.full(m_ref.shape, NEG_INF, F32)
    l_ref[...] = jnp.zeros(l_ref.shape, F32)
    acc_ref[...] = jnp.zeros(acc_ref.shape, F32)
    q = q_ref[...]

    def scores(kstart, slot, masked):
        k = k_ref[pl.ds(kstart, tk), :]
        if masked:
            row = lax.broadcasted_iota(jnp.int32, (tq, tk), 0) + qi * tq
            col = lax.broadcasted_iota(jnp.int32, (tq, tk), 1) + kstart
            visible = col <= row
        for c in range(2):
            s = lax.dot_general(q[:, c * HD_C:(c + 1) * HD_C], k[:, c * HD_C:(c + 1) * HD_C],
                                (((1,), (1,)), ((), ())), preferred_element_type=F32)
            if masked:
                s = jnp.where(visible, s, NEG_INF)
            m_prev = m_ref[c]
            m_new = jnp.maximum(m_prev, jnp.max(s, axis=-1, keepdims=True))
            alpha = jnp.exp2(m_prev - m_new)
            p = jnp.exp2(s - _lane_tile(m_new, tk))
            l_ref[c] = alpha * l_ref[c] + jnp.sum(p, axis=-1, keepdims=True)
            m_ref[c] = m_new
            p_ref[slot, c] = p.astype(BF16)
            a_ref[slot, c] = alpha

    def accumulate(vstart, slot):
        v = v_ref[pl.ds(vstart, tk), :]
        for c in range(2):
            acc_ref[c] = _lane_tile(a_ref[slot, c], DV_C) * acc_ref[c] + jnp.dot(
                p_ref[slot, c], v, preferred_element_type=F32)

    n_full = (qi * tq) // tk
    diag = pl.multiple_of(n_full * tk, tk)
    scores(diag, 0, True)

    def body(i, pending):
        vstart, slot = pending
        kstart = pl.multiple_of(i * tk, tk)
        accumulate(pl.multiple_of(vstart, tk), slot)
        scores(kstart, 1 - slot, False)
        return kstart, 1 - slot

    vstart, slot = lax.fori_loop(0, n_full, body, (diag, jnp.int32(0)))
    accumulate(pl.multiple_of(vstart, tk), slot)

    lam = _diff_lambda(lq1_ref[...], lk1_ref[...], lq2_ref[...], lk2_ref[...], lambda_init)
    o = acc_ref[0] / _lane_tile(l_ref[0], DV_C) - lam * (acc_ref[1] / _lane_tile(l_ref[1], DV_C))
    ms = jnp.mean(o * o, axis=-1, keepdims=True)
    o_ref[...] = (o * lax.rsqrt(ms + LN_EPS) * g_ref[...] * (1.0 - lambda_init)).astype(o_ref.dtype)


def _attn(qkv, prm, l, nb, seq, tq, tk, lambda_init):
    m = qkv.shape[0]
    nq = seq // tq
    lamspec = pl.BlockSpec((None, 1, HD_C), lambda b, h, qi: (l, 0, 0))
    return pl.pallas_call(
        functools.partial(_attn_kernel, lambda_init=lambda_init, tk=tk),
        grid=(nb, H_C, nq),
        in_specs=[
            pl.BlockSpec((tq, DV_C), lambda b, h, qi: (b * nq + qi, h)),
            pl.BlockSpec((seq, DV_C), lambda b, h, qi: (b, H_C + h)),
            pl.BlockSpec((seq, DV_C), lambda b, h, qi: (b, 2 * H_C + h)),
            lamspec, lamspec, lamspec, lamspec,
            pl.BlockSpec((None, 1, DV_C), lambda b, h, qi: (l, 0, 0)),
        ],
        out_specs=pl.BlockSpec((tq, DV_C), lambda b, h, qi: (b * nq + qi, h)),
        out_shape=jax.ShapeDtypeStruct((m, W_C), BF16),
        scratch_shapes=[
            pltpu.VMEM((2, tq, LANES), F32),
            pltpu.VMEM((2, tq, LANES), F32),
            pltpu.VMEM((2, tq, DV_C), F32),
            pltpu.VMEM((2, 2, tq, tk), BF16),
            pltpu.VMEM((2, 2, tq, LANES), F32),
        ],
        compiler_params=_cparams(("arbitrary",) * 3),
        name="attn",
    )(qkv, qkv, qkv, prm["lam_q1"], prm["lam_k1"], prm["lam_q2"], prm["lam_k2"], prm["subln_g"])


def _outproj_kernel(ya_ref, yb_ref, o_ref, x_ref, w_ref, g_ref, b_ref, out_ref, outb_ref, *wb_refs, alpha):
    w = w_ref[...].astype(BF16)
    for wb_ref in wb_refs:
        wb_ref[...] = w
    half = x_ref.shape[0] // 2
    for rows in (pl.ds(0, half), pl.ds(half, half)):
        cat = jnp.concatenate([ya_ref[rows, :].astype(BF16), yb_ref[rows, :].astype(BF16),
                               o_ref[rows, :].astype(BF16)], axis=-1)
        y = jnp.dot(cat, w, preferred_element_type=F32)
        r = _layer_norm(alpha * x_ref[rows, :] + y, g_ref[...], b_ref[...])
        out_ref[rows, :] = r
        outb_ref[rows, :] = r.astype(BF16)


def _outproj(ya, yb, o, x, w, wl, prm, l, tm, alpha, emit_weights):
    m = x.shape[0]
    vec = pl.BlockSpec((None, 1, D_MODEL), lambda i: (l, 0, 0))
    row = pl.BlockSpec((tm, D_MODEL), lambda i: (i, 0))
    wspec = pl.BlockSpec((None, D_MODEL, D_MODEL), lambda i: (wl, 0, 0), pipeline_mode=pl.Buffered(1))
    out_specs = [row, row]
    out_shape = [jax.ShapeDtypeStruct((m, D_MODEL), F32), jax.ShapeDtypeStruct((m, D_MODEL), BF16)]
    if emit_weights:
        out_specs.append(pl.BlockSpec((None, D_MODEL, D_MODEL), lambda i: (0, 0, 0)))
        out_shape.append(jax.ShapeDtypeStruct((1, D_MODEL, D_MODEL), BF16))
    return pl.pallas_call(
        functools.partial(_outproj_kernel, alpha=alpha),
        grid=(m // tm,),
        in_specs=[
            pl.BlockSpec((tm, W_A), lambda i: (i, 0)),
            pl.BlockSpec((tm, W_B), lambda i: (i, 0)),
            pl.BlockSpec((tm, W_C), lambda i: (i, 0)),
            row,
            wspec,
            vec, vec,
        ],
        out_specs=out_specs,
        out_shape=out_shape,
        compiler_params=_cparams(("arbitrary",)),
        name="outproj",
    )(ya, yb, o, x, w, prm["ln1_g"], prm["ln1_b"])


def _swiglu_kernel(x_ref, xb_ref, wg_ref, wu_ref, wd_ref, g_ref, b_ref, out_ref, outb_ref, *rest, alpha):
    *wb_refs, acc_ref = rest
    f = pl.program_id(1)

    @pl.when(f == 0)
    def _():
        acc_ref[...] = jnp.zeros(acc_ref.shape, F32)

    wg, wu, wd = (r[...].astype(BF16) for r in (wg_ref, wu_ref, wd_ref))
    for wb_ref, w in zip(wb_refs, (wg, wu, wd)):
        wb_ref[...] = w
    xb = xb_ref[...]
    hg = jnp.dot(xb, wg, preferred_element_type=F32)
    hu = jnp.dot(xb, wu, preferred_element_type=F32)
    h = (hg * jax.nn.sigmoid(hg)) * hu
    acc_ref[...] += jnp.dot(h.astype(BF16), wd, preferred_element_type=F32)

    @pl.when(f == pl.num_programs(1) - 1)
    def _():
        r = _layer_norm(alpha * x_ref[...] + acc_ref[...], g_ref[...], b_ref[...])
        out_ref[...] = r
        outb_ref[...] = r.astype(BF16)


def _swiglu(x, xb, wg, wu, wd, wl, prm, l, tm, tf, alpha, emit_weights):
    m = x.shape[0]
    vec = pl.BlockSpec((None, 1, D_MODEL), lambda i, f: (l, 0, 0))
    row = pl.BlockSpec((tm, D_MODEL), lambda i, f: (i, 0))
    col_tile = lambda layer: pl.BlockSpec((None, D_MODEL, tf), lambda i, f: (layer, 0, f))
    row_tile = lambda layer: pl.BlockSpec((None, tf, D_MODEL), lambda i, f: (layer, f, 0))
    out_specs = [row, row]
    out_shape = [jax.ShapeDtypeStruct((m, D_MODEL), F32), jax.ShapeDtypeStruct((m, D_MODEL), BF16)]
    if emit_weights:
        out_specs += [col_tile(0), col_tile(0), row_tile(0)]
        out_shape += [jax.ShapeDtypeStruct((1, D_MODEL, D_FF), BF16)] * 2
        out_shape += [jax.ShapeDtypeStruct((1, D_FF, D_MODEL), BF16)]
    return pl.pallas_call(
        functools.partial(_swiglu_kernel, alpha=alpha),
        grid=(m // tm, D_FF // tf),
        in_specs=[row, row, col_tile(wl), col_tile(wl), row_tile(wl), vec, vec],
        out_specs=out_specs,
        out_shape=out_shape,
        scratch_shapes=[pltpu.VMEM((tm, D_MODEL), F32)],
        compiler_params=_cparams(("arbitrary", "arbitrary")),
        name="swiglu",
    )(x, xb, wg, wu, wd, prm["ln2_g"], prm["ln2_b"])


def _sample_mix_kernel(za_ref, cbuf_ref, h0_ref, cw_ref, cb_ref, wa_ref, ba_ref, wx_ref, bx_ref, lam_ref,
                       vng_ref, vnb_ref, w4_ref, b4_ref,
                       ya_ref, yb_ref, hl_ref, cout_ref, vn_ref):
    nt = vn_ref.shape[0]
    nb = h0_ref.shape[0]
    zc = 4 * MIX_W // LANES
    gc = MIX_W // LANES

    def rows(t, grp):
        return jnp.concatenate(
            [za_ref[pl.ds(t * zc + grp * gc + c, nb, stride=nt * zc), :] for c in range(gc)], axis=-1)

    def put_rows(ref, t, val):
        for c in range(gc):
            ref[pl.ds(t * gc + c, nb, stride=nt * gc), :] = val[:, c * LANES:(c + 1) * LANES]

    xa = [rows(t, 0) for t in range(nt)]
    xp = [cbuf_ref[j] for j in range(CONV_W - 1)] + xa
    cw = cw_ref[...]
    xc = []
    for t in range(nt):
        acc = cb_ref[...] + cw[0:1] * xp[t]
        for j in range(1, CONV_W):
            acc = acc + cw[j:j + 1] * xp[t + j]
        xc.append(acc)
    a, b = _rglru_coeffs(jnp.concatenate(xc, axis=0), wa_ref, ba_ref, wx_ref, bx_ref, lam_ref)
    h = h0_ref[...]
    for t in range(nt):
        h = a[t * nb:(t + 1) * nb] * h + b[t * nb:(t + 1) * nb]
        put_rows(ya_ref, t, h * jax.nn.gelu(rows(t, 1)))
    hl_ref[...] = h
    for j in range(CONV_W - 1):
        cout_ref[j] = xp[nt + j]

    vn = []
    for t in range(nt):
        v = _layer_norm(jax.nn.gelu(rows(t, 3)), vng_ref[...], vnb_ref[...])
        vn_ref[t] = v
        vn.append(v)
    for t in range(nt):
        mix = b4_ref[t:t + 1]
        for s in range(t + 1):
            mix = mix + w4_ref[t, s:s + 1] * vn[s]
        put_rows(yb_ref, t, jax.nn.gelu(rows(t, 2)) * mix)


def _sample_mix(za, cbuf, h0, prm, l, nb, nt):
    m = nb * nt
    full = lambda shape: pl.BlockSpec(shape, lambda i: (0,) * len(shape))
    lsel = lambda shape: pl.BlockSpec((None,) + shape, lambda i: (l,) + (0,) * len(shape))
    return pl.pallas_call(
        _sample_mix_kernel,
        grid=(1,),
        in_specs=[
            full((m * 4 * MIX_W // LANES, LANES)),
            lsel((CONV_W - 1, nb, W_A)),
            lsel((nb, W_A)),
            lsel((CONV_W, W_A)), lsel((1, W_A)),
            lsel((H_A, BW_A, BW_A)), lsel((1, W_A)),
            lsel((H_A, BW_A, BW_A)), lsel((1, W_A)),
            lsel((1, W_A)),
            lsel((1, W_B)), lsel((1, W_B)),
            lsel((nt, nt, W_B)), lsel((nt, W_B)),
        ],
        out_specs=[
            full((m * W_A // LANES, LANES)), full((m * W_B // LANES, LANES)), full((nb, W_A)),
            full((CONV_W - 1, nb, W_A)), full((nt, nb, W_B)),
        ],
        out_shape=[
            jax.ShapeDtypeStruct((m * W_A // LANES, LANES), F32),
            jax.ShapeDtypeStruct((m * W_B // LANES, LANES), F32),
            jax.ShapeDtypeStruct((nb, W_A), F32),
            jax.ShapeDtypeStruct((CONV_W - 1, nb, W_A), F32),
            jax.ShapeDtypeStruct((nt, nb, W_B), F32),
        ],
        compiler_params=_cparams(("arbitrary",)),
        name="sample_mix",
    )(za, cbuf, h0, prm["conv_w"], prm["conv_b"], prm["w_a"], prm["b_a"], prm["w_x"], prm["b_x"],
      prm["lru_lam"], prm["vn_g"], prm["vn_b"], prm["w4"], prm["b4"])


def _expand_queries(q):
    hc = lax.broadcasted_iota(jnp.int32, (2 * H_C, W_C), 0)
    blk = lax.broadcasted_iota(jnp.int32, (2 * H_C, W_C), 1) // HD_C
    keep = hc == blk
    return jnp.concatenate(
        [jnp.where(keep, jnp.broadcast_to(q[t:t + 1], (2 * H_C, W_C)), 0.0) for t in range(q.shape[0])],
        axis=0)


def _page_keys(ref, page):
    return jnp.concatenate(
        [ref[pl.ds(_k_piece(hc), page, stride=HEAD_ROWS), :] for hc in range(2 * H_C)], axis=-1)


def _page_values(ref, page):
    return jnp.concatenate(
        [ref[pl.ds(_v_piece(h, half), page, stride=HEAD_ROWS), :] for h in range(H_C) for half in range(2)],
        axis=-1)


def _paged_attn_kernel(pt_ref, q_ref, kn_ref, vn_ref, lq1_ref, lk1_ref, lq2_ref, lk2_ref, g_ref, *rest,
                       pages_per_step, lambda_init):
    del pt_ref
    k_refs = rest[:pages_per_step]
    v_refs = rest[pages_per_step:2 * pages_per_step]
    o_ref, qe_ref, m_ref, l_ref, acc_ref = rest[2 * pages_per_step:]
    j = pl.program_id(1)
    nt = q_ref.shape[0]
    page = k_refs[0].shape[0] // HEAD_ROWS

    @pl.when(j == 0)
    def _():
        qe_ref[...] = _expand_queries(q_ref[...]).astype(BF16)
        m_ref[...] = jnp.full(m_ref.shape, NEG_INF, F32)
        l_ref[...] = jnp.zeros(l_ref.shape, F32)
        acc_ref[...] = jnp.zeros(acc_ref.shape, F32)

    qe = qe_ref[...]
    s = jnp.concatenate(
        [lax.dot_general(qe, _page_keys(k_refs[p], page).astype(BF16), (((1,), (1,)), ((), ())),
                         preferred_element_type=F32) for p in range(pages_per_step)], axis=-1)
    m_prev = m_ref[...]
    m_new = jnp.maximum(m_prev, jnp.max(s, axis=-1, keepdims=True))
    alpha = jnp.exp2(m_prev - m_new)
    p = jnp.exp2(s - m_new)
    l_ref[...] = alpha * l_ref[...] + jnp.sum(p, axis=-1, keepdims=True)
    pb = p.astype(BF16)
    pv = jnp.dot(pb[:, 0:page], _page_values(v_refs[0], page).astype(BF16), preferred_element_type=F32)
    for i in range(1, pages_per_step):
        pv = pv + jnp.dot(pb[:, i * page:(i + 1) * page], _page_values(v_refs[i], page).astype(BF16),
                          preferred_element_type=F32)
    acc_ref[...] = alpha * acc_ref[...] + pv
    m_ref[...] = m_new

    @pl.when(j == pl.num_programs(1) - 1)
    def _():
        qe32 = _expand_queries(q_ref[...])
        trow = lax.broadcasted_iota(jnp.int32, (nt * 2 * H_C, 1), 0) // (2 * H_C)
        kn = kn_ref[...]
        vn = vn_ref[...]
        sc = [jnp.where(trow >= t, jnp.sum(qe32 * kn[t:t + 1], axis=-1, keepdims=True), NEG_INF)
              for t in range(nt)]
        m_prev = m_ref[...]
        m_new = m_prev
        for t in range(nt):
            m_new = jnp.maximum(m_new, sc[t])
        alpha = jnp.exp2(m_prev - m_new)
        lsum = alpha * l_ref[...]
        acc = alpha * acc_ref[...]
        for t in range(nt):
            pt = jnp.exp2(sc[t] - m_new)
            lsum = lsum + pt
            acc = acc + pt * vn[t:t + 1]
        accn = acc / lsum
        lam = _diff_lambda(lq1_ref[...], lk1_ref[...], lq2_ref[...], lk2_ref[...], lambda_init)
        hc = lax.broadcasted_iota(jnp.int32, (nt * 2 * H_C, 1), 0) % (2 * H_C)
        outs = []
        for h in range(H_C):
            w = jnp.where(hc == 2 * h, 1.0, jnp.where(hc == 2 * h + 1, -lam, 0.0))
            o = jnp.sum((accn[:, h * DV_C:(h + 1) * DV_C] * w).reshape(nt, 2 * H_C, DV_C), axis=1)
            ms = jnp.mean(o * o, axis=-1, keepdims=True)
            outs.append(o * lax.rsqrt(ms + LN_EPS) * g_ref[...] * (1.0 - lambda_init))
        o_ref[...] = jnp.concatenate(outs, axis=-1)


def _paged_attn(page_table, qkv3, cache_k, cache_v, prm, l, pages_per_step, lambda_init):
    nb, nt, _ = qkv3.shape
    n_pages = page_table.shape[0] // nb
    page_rows = cache_k.shape[2]
    lamspec = pl.BlockSpec((None, 1, HD_C), lambda b, j, pt: (l, 0, 0))

    def page_spec(p):
        return pl.BlockSpec((None, None, page_rows, LANES),
                            lambda b, j, pt: (l, pt[b * n_pages + j * pages_per_step + p], 0, 0))

    grid_spec = pltpu.PrefetchScalarGridSpec(
        num_scalar_prefetch=1,
        grid=(nb, n_pages // pages_per_step),
        in_specs=[
            pl.BlockSpec((None, nt, W_C), lambda b, j, pt: (b, 0, 0)),
            pl.BlockSpec((None, nt, W_C), lambda b, j, pt: (b, 0, 1)),
            pl.BlockSpec((None, nt, W_C), lambda b, j, pt: (b, 0, 2)),
            lamspec, lamspec, lamspec, lamspec,
            pl.BlockSpec((None, 1, DV_C), lambda b, j, pt: (l, 0, 0)),
        ] + [page_spec(p) for p in range(pages_per_step)] * 2,
        out_specs=pl.BlockSpec((None, nt, W_C), lambda b, j, pt: (b, 0, 0)),
        scratch_shapes=[
            pltpu.VMEM((nt * 2 * H_C, W_C), BF16),
            pltpu.VMEM((nt * 2 * H_C, 1), F32),
            pltpu.VMEM((nt * 2 * H_C, 1), F32),
            pltpu.VMEM((nt * 2 * H_C, W_C), F32),
        ],
    )
    return pl.pallas_call(
        functools.partial(_paged_attn_kernel, pages_per_step=pages_per_step, lambda_init=lambda_init),
        grid_spec=grid_spec,
        out_shape=jax.ShapeDtypeStruct((nb, nt, W_C), F32),
        compiler_params=_cparams(("arbitrary", "arbitrary")),
        name="paged_attn",
    )(page_table, qkv3, qkv3, qkv3, prm["lam_q1"], prm["lam_k1"], prm["lam_q2"], prm["lam_k2"],
      prm["subln_g"], *([cache_k] * pages_per_step), *([cache_v] * pages_per_step))


def _values_to_head_rows(v):
    lead = v.shape[:-3]
    tokens = v.shape[-3]
    v = v.reshape(lead + (tokens, H_C, 2, LANES))
    v = jnp.swapaxes(v, -3, -2)
    return v.reshape(lead + (tokens * HEAD_ROWS, LANES))


def _head_rows_to_values(r, lead):
    depth = r.shape[0]
    v = r.reshape((depth,) + lead + (2, H_C, LANES))
    v = jnp.swapaxes(v, -3, -2)
    return v.reshape((depth,) + lead + (H_C, DV_C))


def _largest_tile(n, cap, mult):
    t = min(n, cap)
    while n % t or t % mult:
        t -= 1
    return t


def kernel(x_prompt, x_sample, cache_k, cache_v, state_lru_h, state_conv, page_table, w_in, w_out, conv_w,
           conv_b, w_a, b_a, w_x, b_x, lru_lam, vn_g, vn_b, w_s, b_s, lam_q1, lam_k1, lam_q2, lam_k2,
           subln_g, ln1_g, ln1_b, ln2_g, ln2_b, w_gate, w_up, w_down):
    depth = w_in.shape[0]
    nbp, seq, _ = x_prompt.shape
    nbs, nts, _ = x_sample.shape
    n_pool, page = cache_k.shape[1], cache_k.shape[2]
    n_pages = page_table.shape[1]
    alpha = (2.0 * depth) ** 0.25
    mp, ms = nbp * seq, nbs * nts

    row3 = lambda a: a.reshape(depth, 1, a.shape[-1])
    prm = {
        "w_a": w_a.astype(BF16), "w_x": w_x.astype(BF16),
        "conv_w": conv_w, "conv_b": row3(conv_b), "b_a": row3(b_a), "b_x": row3(b_x),
        "lru_lam": row3(lru_lam), "vn_g": row3(vn_g), "vn_b": row3(vn_b), "w_s": w_s,
        "bs_rows": jnp.repeat(jnp.transpose(b_s, (0, 2, 1)), GW_B, axis=-1),
        "w4": jnp.repeat(jnp.transpose(w_s[:, :, :nts, :nts], (0, 2, 3, 1)), GW_B, axis=-1),
        "b4": jnp.repeat(jnp.transpose(b_s[:, :, :nts], (0, 2, 1)), GW_B, axis=-1),
        "lam_q1": row3(lam_q1), "lam_k1": row3(lam_k1), "lam_q2": row3(lam_q2), "lam_k2": row3(lam_k2),
        "subln_g": row3(subln_g), "ln1_g": row3(ln1_g), "ln1_b": row3(ln1_b),
        "ln2_g": row3(ln2_g), "ln2_b": row3(ln2_b),
    }
    ck = cache_k.reshape(depth, n_pool, page * HEAD_ROWS, LANES)
    cv = _values_to_head_rows(cache_v)
    pt_flat = page_table.reshape(-1)
    sconv = jnp.transpose(state_conv, (0, 2, 1, 3))
    zero_cbuf = jnp.zeros((nbp, SUBLANES, W_A), F32)
    zero_h = jnp.zeros((nbp, 1, W_A), F32)

    tm_in = _largest_tile(mp, 1024, CHUNK)
    tm_p = _largest_tile(mp, 512, CHUNK)
    tt_a = _largest_tile(seq, 256, SUBLANES)
    tk = _largest_tile(seq, 512, CHUNK)
    tq = _largest_tile(tk, 256, CHUNK)
    tf = _largest_tile(D_FF, 512, 128)
    pps = _largest_tile(n_pages, 16, 1)

    xp = x_prompt.reshape(mp, D_MODEL)
    xs = x_sample.reshape(ms, D_MODEL)
    xp_b, xs_b = xp.astype(BF16), xs.astype(BF16)
    kbuf_p = jnp.zeros((depth, mp * HEAD_ROWS, LANES), F32)
    vbuf_p = jnp.zeros((depth, mp * HEAD_ROWS, LANES), F32)
    kbuf_s = jnp.zeros((depth, ms * HEAD_ROWS, LANES), F32)
    vbuf_s = jnp.zeros((depth, ms * HEAD_ROWS, LANES), F32)
    outs = [[] for _ in range(5)]
    for l in range(depth):
        lambda_init = 0.8 - 0.6 * math.exp(-0.3 * l)
        za_s, qkv_s, kbuf_s, vbuf_s, w_in_b = _inproj(xs_b, w_in, l, kbuf_s, vbuf_s, l, ms, F32, True)
        ya_s, yb_s, h_s, c_s, vn_s = _sample_mix(za_s.reshape(-1, LANES), sconv, state_lru_h, prm, l, nbs,
                                                  nts)
        ya_s, yb_s = ya_s.reshape(ms, W_A), yb_s.reshape(ms, W_B)
        o_s = _paged_attn(pt_flat, qkv_s.reshape(nbs, nts, 3 * W_C), ck, cv, prm, l, pps, lambda_init)
        xs, xs_b, w_out_b = _outproj(ya_s, yb_s, o_s.reshape(ms, W_C), xs, w_out, l, prm, l, ms, alpha, True)
        xs, xs_b, wg_b, wu_b, wd_b = _swiglu(xs, xs_b, w_gate, w_up, w_down, l, prm, l, ms, tf, alpha, True)
        za, qkv, kbuf_p, vbuf_p = _inproj(xp_b, w_in_b, 0, kbuf_p, vbuf_p, l, tm_in, BF16, False)
        ya, h_p, c_p = _mix_a(za, zero_cbuf, zero_h, prm, l, nbp, tt_a)
        yb = _mix_b(za, prm, l, tm_p)
        o = _attn(qkv, prm, l, nbp, seq, tq, tk, lambda_init)
        xp, xp_b = _outproj(ya, yb, o, xp, w_out_b, 0, prm, l, tm_p, alpha, False)
        xp, xp_b = _swiglu(xp, xp_b, wg_b, wu_b, wd_b, 0, prm, l, tm_p, tf, alpha, False)

        for lst, val in zip(outs, (h_p, h_s, c_p, c_s, vn_s)):
            lst.append(val)

    hp, hs, cp, cs, chv = [jnp.stack(v) for v in outs]
    return (
        xp.reshape(nbp, seq, D_MODEL),
        xs.reshape(nbs, nts, D_MODEL),
        kbuf_p.reshape(depth, nbp, seq, H_C, 2, HD_C),
        _head_rows_to_values(vbuf_p, (nbp, seq)),
        kbuf_s.reshape(depth, nbs, nts, H_C, 2, HD_C),
        _head_rows_to_values(vbuf_s, (nbs, nts)),
        hp.reshape(depth, nbp, W_A),
        hs,
        cp[:, :, SUBLANES - (CONV_W - 1):, :],
        jnp.transpose(cs, (0, 2, 1, 3)),
        jnp.transpose(chv, (0, 2, 1, 3)),
    )
```

```python
import functools
import math

import jax
import jax.numpy as jnp
from jax import lax
from jax.experimental import pallas as pl
from jax.experimental.pallas import tpu as pltpu

F32 = jnp.float32
BF16 = jnp.bfloat16

D_MODEL = 2048
W_A = 512
H_A = 4
BW_A = 128
CONV_W = 4
RG_C = 8.0
W_B = 512
G_B = 4
GW_B = 128
CHUNK = 128
W_C = 1024
H_C = 4
HD_C = 128
DV_C = 256
W_IN = 2 * W_A + 2 * W_B + 3 * W_C
D_FF = 5632
LN_EPS = 1e-5
NEG_INF = -1e30
QK_SCALE = HD_C ** -0.5 * math.log2(math.e)

SUBLANES = 8
LANES = 128
COL_TILE = 1024
MIX_W = 512
HEAD_ROWS = W_C // LANES
VMEM_LIMIT = 56 * 1024 * 1024


def _cparams(sem):
    return pltpu.CompilerParams(dimension_semantics=sem, vmem_limit_bytes=VMEM_LIMIT)


def _layer_norm(r, g, b):
    mu = jnp.mean(r, axis=-1, keepdims=True)
    d = r - mu
    var = jnp.mean(d * d, axis=-1, keepdims=True)
    return d * lax.rsqrt(var + LN_EPS) * g + b


def _softplus(x):
    return jnp.maximum(x, 0.0) + jnp.log1p(jnp.exp(-jnp.abs(x)))


def _neg_expm1_nonpos(x):
    t = jnp.tanh(0.5 * x)
    return -2.0 * t / (1.0 - t)


def _diff_lambda(lq1, lk1, lq2, lk2, lambda_init):
    s1 = jnp.sum(lq1 * lk1, axis=-1, keepdims=True)
    s2 = jnp.sum(lq2 * lk2, axis=-1, keepdims=True)
    return jnp.exp(s1) - jnp.exp(s2) + lambda_init


def _lane_tile(x, width):
    return jnp.concatenate([x] * (width // LANES), axis=-1)


def _k_piece(hc):
    return hc


def _v_piece(h, half):
    return half * H_C + h


def _inproj_kernel(x_ref, w_ref, kin_ref, vin_ref, za_ref, qkv_ref, kf_ref, vf_ref, *wb_refs):
    del kin_ref, vin_ref
    j = pl.program_id(1)
    tm = x_ref.shape[0]
    for wb_ref in wb_refs:
        wb_ref[...] = w_ref[...].astype(BF16)

    def project():
        return jnp.dot(x_ref[...], w_ref[...].astype(BF16), preferred_element_type=F32)

    @pl.when(j < 2)
    def _():
        za_ref[...] = project()

    @pl.when(j == 2)
    def _():
        qkv_ref[...] = (project() * QK_SCALE).astype(qkv_ref.dtype)

    @pl.when(j == 3)
    def _():
        z = project()
        qkv_ref[...] = z.astype(qkv_ref.dtype)
        for hc in range(2 * H_C):
            kf_ref[pl.ds(_k_piece(hc), tm, stride=HEAD_ROWS), :] = z[:, hc * HD_C:(hc + 1) * HD_C]

    @pl.when(j == 4)
    def _():
        z = project()
        qkv_ref[...] = z.astype(qkv_ref.dtype)
        for h in range(H_C):
            for half in range(2):
                c0 = h * DV_C + half * LANES
                vf_ref[pl.ds(_v_piece(h, half), tm, stride=HEAD_ROWS), :] = z[:, c0:c0 + LANES]


def _inproj(xb, w, wl, kbuf, vbuf, l, tm, qkv_dtype, emit_weights):
    m = xb.shape[0]
    nj = W_IN // COL_TILE
    kv_spec = pl.BlockSpec((None, tm * HEAD_ROWS, LANES), lambda i, j: (l, i, 0))
    out_specs = [
        pl.BlockSpec((tm, COL_TILE), lambda i, j: (i, jnp.minimum(j, 1))),
        pl.BlockSpec((tm, COL_TILE), lambda i, j: (i, jnp.clip(j - 2, 0, 2))),
        kv_spec, kv_spec,
    ]
    out_shape = [
        jax.ShapeDtypeStruct((m, 4 * MIX_W), F32),
        jax.ShapeDtypeStruct((m, 3 * W_C), qkv_dtype),
        jax.ShapeDtypeStruct(kbuf.shape, F32),
        jax.ShapeDtypeStruct(vbuf.shape, F32),
    ]
    if emit_weights:
        out_specs.append(pl.BlockSpec((None, D_MODEL, COL_TILE), lambda i, j: (0, 0, j)))
        out_shape.append(jax.ShapeDtypeStruct((1, D_MODEL, W_IN), BF16))
    return pl.pallas_call(
        _inproj_kernel,
        grid=(m // tm, nj),
        in_specs=[
            pl.BlockSpec((tm, D_MODEL), lambda i, j: (i, 0)),
            pl.BlockSpec((None, D_MODEL, COL_TILE), lambda i, j: (wl, 0, j)),
            pl.BlockSpec(memory_space=pl.ANY),
            pl.BlockSpec(memory_space=pl.ANY),
        ],
        out_specs=out_specs,
        out_shape=out_shape,
        input_output_aliases={2: 2, 3: 3},
        compiler_params=_cparams(("arbitrary", "arbitrary")),
        name="inproj",
    )(xb, w, kbuf, vbuf)


def _shift_rows(x, s, fill):
    n, w = x.shape
    if s % SUBLANES == 0:
        return jnp.concatenate([jnp.full((s, w), fill, x.dtype), x[:n - s]], axis=0)
    row = lax.broadcasted_iota(jnp.int32, x.shape, 0)
    return jnp.where(row < s, fill, pltpu.roll(x, s, 0))


def _block_gate(xcb, w_ref, b_ref):
    parts = [jnp.dot(xcb[:, h * BW_A:(h + 1) * BW_A], w_ref[h], preferred_element_type=F32)
             for h in range(H_A)]
    return jax.nn.sigmoid(jnp.concatenate(parts, axis=-1) + b_ref[...])


def _rglru_coeffs(xc, wa_ref, ba_ref, wx_ref, bx_ref, lam_ref):
    xcb = xc.astype(BF16)
    r = _block_gate(xcb, wa_ref, ba_ref)
    gi = _block_gate(xcb, wx_ref, bx_ref)
    log_a = -RG_C * r * _softplus(-lam_ref[...])
    a = jnp.exp(log_a)
    b = jnp.sqrt(_neg_expm1_nonpos(2.0 * log_a)) * (gi * xc)
    return a, b


def _mix_a_kernel(xa_ref, ga_ref, cbuf_ref, h0_ref, cw_ref, cb_ref, wa_ref, ba_ref, wx_ref, bx_ref,
                  lam_ref, ya_ref, hl_ref, cout_ref, tail_ref, hc_ref):
    t = pl.program_id(1)
    tt = xa_ref.shape[0]

    @pl.when(t == 0)
    def _():
        tail_ref[...] = cbuf_ref[...]
        hc_ref[...] = h0_ref[...]

    xa = xa_ref[...]
    tail = tail_ref[...]
    row8 = lax.broadcasted_iota(jnp.int32, tail.shape, 0)

    def delayed(k):
        r = pltpu.roll(xa, k, 0)
        first = jnp.where(row8 < k, pltpu.roll(tail, k, 0), r[0:SUBLANES])
        return jnp.concatenate([first, r[SUBLANES:]], axis=0)

    cw = cw_ref[...]
    xc = cb_ref[...] + cw[3:4] * xa
    for k in range(1, CONV_W):
        xc = xc + cw[CONV_W - 1 - k:CONV_W - k] * delayed(k)
    tail_ref[...] = xa[tt - SUBLANES:tt]

    a, b = _rglru_coeffs(xc, wa_ref, ba_ref, wx_ref, bx_ref, lam_ref)
    s = 1
    while s < tt:
        b = b + a * _shift_rows(b, s, 0.0)
        a = a * _shift_rows(a, s, 1.0)
        s *= 2
    h = a * hc_ref[...] + b
    hc_ref[...] = h[tt - 1:tt]
    ya_ref[...] = (h * jax.nn.gelu(ga_ref[...])).astype(ya_ref.dtype)

    @pl.when(t == pl.num_programs(1) - 1)
    def _():
        hl_ref[...] = h[tt - 1:tt]
        cout_ref[...] = xa[tt - SUBLANES:tt]


def _mix_a(za, cbuf8, h0, prm, l, nb, tt):
    m = za.shape[0]
    nt = m // nb // tt
    vec = pl.BlockSpec((None, 1, W_A), lambda b, t: (l, 0, 0))
    blk = pl.BlockSpec((None, H_A, BW_A, BW_A), lambda b, t: (l, 0, 0, 0))
    return pl.pallas_call(
        _mix_a_kernel,
        grid=(nb, nt),
        in_specs=[
            pl.BlockSpec((tt, W_A), lambda b, t: (b * nt + t, 0)),
            pl.BlockSpec((tt, W_A), lambda b, t: (b * nt + t, 1)),
            pl.BlockSpec((None, SUBLANES, W_A), lambda b, t: (b, 0, 0)),
            pl.BlockSpec((None, 1, W_A), lambda b, t: (b, 0, 0)),
            pl.BlockSpec((None, CONV_W, W_A), lambda b, t: (l, 0, 0)),
            vec, blk, vec, blk, vec, vec,
        ],
        out_specs=[
            pl.BlockSpec((tt, W_A), lambda b, t: (b * nt + t, 0)),
            pl.BlockSpec((None, 1, W_A), lambda b, t: (b, 0, 0)),
            pl.BlockSpec((None, SUBLANES, W_A), lambda b, t: (b, 0, 0)),
        ],
        out_shape=[
            jax.ShapeDtypeStruct((m, W_A), BF16),
            jax.ShapeDtypeStruct((nb, 1, W_A), F32),
            jax.ShapeDtypeStruct((nb, SUBLANES, W_A), F32),
        ],
        scratch_shapes=[pltpu.VMEM((SUBLANES, W_A), F32), pltpu.VMEM((1, W_A), F32)],
        compiler_params=_cparams(("arbitrary", "arbitrary")),
        name="mix_a",
    )(za, za, cbuf8, h0, prm["conv_w"], prm["conv_b"], prm["w_a"], prm["b_a"], prm["w_x"], prm["b_x"],
      prm["lru_lam"])


def _mix_b_kernel(ub_ref, vb_ref, ws_ref, bs_ref, g_ref, b_ref, yb_ref):
    tt = ub_ref.shape[0]
    vn = _layer_norm(jax.nn.gelu(vb_ref[...]), g_ref[...], b_ref[...]).astype(BF16)
    row = lax.broadcasted_iota(jnp.int32, (CHUNK, CHUNK), 0)
    col = lax.broadcasted_iota(jnp.int32, (CHUNK, CHUNK), 1)
    wt = [jnp.where(col <= row, ws_ref[g], 0.0).astype(BF16) for g in range(G_B)]
    bias = bs_ref[...]
    chunks = []
    for c in range(tt // CHUNK):
        vc = vn[c * CHUNK:(c + 1) * CHUNK]
        parts = [jnp.dot(wt[g], vc[:, g * GW_B:(g + 1) * GW_B], preferred_element_type=F32)
                 for g in range(G_B)]
        chunks.append(jnp.concatenate(parts, axis=-1) + bias)
    mix = jnp.concatenate(chunks, axis=0)
    yb_ref[...] = (jax.nn.gelu(ub_ref[...]) * mix).astype(yb_ref.dtype)


def _mix_b(za, prm, l, tt):
    m = za.shape[0]
    return pl.pallas_call(
        _mix_b_kernel,
        grid=(m // tt,),
        in_specs=[
            pl.BlockSpec((tt, W_B), lambda i: (i, 2)),
            pl.BlockSpec((tt, W_B), lambda i: (i, 3)),
            pl.BlockSpec((None, G_B, CHUNK, CHUNK), lambda i: (l, 0, 0, 0)),
            pl.BlockSpec((None, CHUNK, W_B), lambda i: (l, 0, 0)),
            pl.BlockSpec((None, 1, W_B), lambda i: (l, 0, 0)),
            pl.BlockSpec((None, 1, W_B), lambda i: (l, 0, 0)),
        ],
        out_specs=pl.BlockSpec((tt, W_B), lambda i: (i, 0)),
        out_shape=jax.ShapeDtypeStruct((m, W_B), BF16),
        compiler_params=_cparams(("arbitrary",)),
        name="mix_b",
    )(za, za, prm["w_s"], prm["bs_rows"], prm["vn_g"], prm["vn_b"])


def _attn_kernel(q_ref, k_ref, v_ref, lq1_ref, lk1_ref, lq2_ref, lk2_ref, g_ref, o_ref,
                 m_ref, l_ref, acc_ref, p_ref, a_ref, *, lambda_init, tk):
    qi = pl.program_id(2)
    tq = q_ref.shape[0]
    m_ref[...] = jnp.full(m_ref.shape, NEG_INF, F32)
    l_ref[...] = jnp.zeros(l_ref.shape, F32)
    acc_ref[...] = jnp.zeros(acc_ref.shape, F32)
    q = q_ref[...]

    def scores(kstart, slot, masked):
        k = k_ref[pl.ds(kstart, tk), :]
        if masked:
            row = lax.broadcasted_iota(jnp.int32, (tq, tk), 0) + qi * tq
            col = lax.broadcasted_iota(jnp.int32, (tq, tk), 1) + kstart
            visible = col <= row
        for c in range(2):
            s = lax.dot_general(q[:, c * HD_C:(c + 1) * HD_C], k[:, c * HD_C:(c + 1) * HD_C],
                                (((1,), (1,)), ((), ())), preferred_element_type=F32)
            if masked:
                s = jnp.where(visible, s, NEG_INF)
            m_prev = m_ref[c]
            m_new = jnp.maximum(m_prev, jnp.max(s, axis=-1, keepdims=True))
            alpha = jnp.exp2(m_prev - m_new)
            p = jnp.exp2(s - _lane_tile(m_new, tk))
            l_ref[c] = alpha * l_ref[c] + jnp.sum(p, axis=-1, keepdims=True)
            m_ref[c] = m_new
            p_ref[slot, c] = p.astype(BF16)
            a_ref[slot, c] = alpha

    def accumulate(vstart, slot):
        v = v_ref[pl.ds(vstart, tk), :]
        for c in range(2):
            acc_ref[c] = _lane_tile(a_ref[slot, c], DV_C) * acc_ref[c] + jnp.dot(
                p_ref[slot, c], v, preferred_element_type=F32)

    n_full = (qi * tq) // tk
    diag = pl.multiple_of(n_full * tk, tk)
    scores(diag, 0, True)

    def body(i, pending):
        vstart, slot = pending
        kstart = pl.multiple_of(i * tk, tk)
        accumulate(pl.multiple_of(vstart, tk), slot)
        scores(kstart, 1 - slot, False)
        return kstart, 1 - slot

    def pair_body(i, pending):
        return body(2 * i + 1, body(2 * i, pending))

    n_pairs = n_full // 2
    pending = lax.fori_loop(0, n_pairs, pair_body, (diag, jnp.int32(0)))
    vstart, slot = lax.fori_loop(2 * n_pairs, n_full, body, pending)
    accumulate(pl.multiple_of(vstart, tk), slot)

    lam = _diff_lambda(lq1_ref[...], lk1_ref[...], lq2_ref[...], lk2_ref[...], lambda_init)
    o = acc_ref[0] / _lane_tile(l_ref[0], DV_C) - lam * (acc_ref[1] / _lane_tile(l_ref[1], DV_C))
    ms = jnp.mean(o * o, axis=-1, keepdims=True)
    o_ref[...] = (o * lax.rsqrt(ms + LN_EPS) * g_ref[...] * (1.0 - lambda_init)).astype(o_ref.dtype)


def _attn(qkv, prm, l, nb, seq, tq, tk, lambda_init):
    m = qkv.shape[0]
    nq = seq // tq
    lamspec = pl.BlockSpec((None, 1, HD_C), lambda b, h, qi: (l, 0, 0))
    return pl.pallas_call(
        functools.partial(_attn_kernel, lambda_init=lambda_init, tk=tk),
        grid=(nb, H_C, nq),
        in_specs=[
            pl.BlockSpec((tq, DV_C), lambda b, h, qi: (b * nq + qi, h)),
            pl.BlockSpec((seq, DV_C), lambda b, h, qi: (b, H_C + h)),
            pl.BlockSpec((seq, DV_C), lambda b, h, qi: (b, 2 * H_C + h)),
            lamspec, lamspec, lamspec, lamspec,
            pl.BlockSpec((None, 1, DV_C), lambda b, h, qi: (l, 0, 0)),
        ],
        out_specs=pl.BlockSpec((tq, DV_C), lambda b, h, qi: (b * nq + qi, h)),
        out_shape=jax.ShapeDtypeStruct((m, W_C), BF16),
        scratch_shapes=[
            pltpu.VMEM((2, tq, LANES), F32),
            pltpu.VMEM((2, tq, LANES), F32),
            pltpu.VMEM((2, tq, DV_C), F32),
            pltpu.VMEM((2, 2, tq, tk), BF16),
            pltpu.VMEM((2, 2, tq, LANES), F32),
        ],
        compiler_params=_cparams(("arbitrary",) * 3),
        name="attn",
    )(qkv, qkv, qkv, prm["lam_q1"], prm["lam_k1"], prm["lam_q2"], prm["lam_k2"], prm["subln_g"])


def _outproj_kernel(ya_ref, yb_ref, o_ref, x_ref, w_ref, g_ref, b_ref, out_ref, outb_ref, *wb_refs, alpha):
    w = w_ref[...].astype(BF16)
    for wb_ref in wb_refs:
        wb_ref[...] = w
    half = x_ref.shape[0] // 2
    for rows in (pl.ds(0, half), pl.ds(half, half)):
        cat = jnp.concatenate([ya_ref[rows, :].astype(BF16), yb_ref[rows, :].astype(BF16),
                               o_ref[rows, :].astype(BF16)], axis=-1)
        y = jnp.dot(cat, w, preferred_element_type=F32)
        r = _layer_norm(alpha * x_ref[rows, :] + y, g_ref[...], b_ref[...])
        out_ref[rows, :] = r
        outb_ref[rows, :] = r.astype(BF16)


def _outproj(ya, yb, o, x, w, wl, prm, l, tm, alpha, emit_weights):
    m = x.shape[0]
    vec = pl.BlockSpec((None, 1, D_MODEL), lambda i: (l, 0, 0))
    row = pl.BlockSpec((tm, D_MODEL), lambda i: (i, 0))
    wspec = pl.BlockSpec((None, D_MODEL, D_MODEL), lambda i: (wl, 0, 0), pipeline_mode=pl.Buffered(1))
    out_specs = [row, row]
    out_shape = [jax.ShapeDtypeStruct((m, D_MODEL), F32), jax.ShapeDtypeStruct((m, D_MODEL), BF16)]
    if emit_weights:
        out_specs.append(pl.BlockSpec((None, D_MODEL, D_MODEL), lambda i: (0, 0, 0)))
        out_shape.append(jax.ShapeDtypeStruct((1, D_MODEL, D_MODEL), BF16))
    return pl.pallas_call(
        functools.partial(_outproj_kernel, alpha=alpha),
        grid=(m // tm,),
        in_specs=[
            pl.BlockSpec((tm, W_A), lambda i: (i, 0)),
            pl.BlockSpec((tm, W_B), lambda i: (i, 0)),
            pl.BlockSpec((tm, W_C), lambda i: (i, 0)),
            row,
            wspec,
            vec, vec,
        ],
        out_specs=out_specs,
        out_shape=out_shape,
        compiler_params=_cparams(("arbitrary",)),
        name="outproj",
    )(ya, yb, o, x, w, prm["ln1_g"], prm["ln1_b"])


def _swiglu_kernel(x_ref, xb_ref, wg_ref, wu_ref, wd_ref, g_ref, b_ref, out_ref, outb_ref, *rest, alpha):
    *wb_refs, acc_ref = rest
    f = pl.program_id(1)

    @pl.when(f == 0)
    def _():
        acc_ref[...] = jnp.zeros(acc_ref.shape, F32)

    for wb_ref, w_ref in zip(wb_refs, (wg_ref, wu_ref, wd_ref)):
        wb_ref[...] = w_ref[...].astype(BF16)

    def ffn(rows):
        xb = xb_ref[rows, :]
        hg = jnp.dot(xb, wg_ref[...].astype(BF16), preferred_element_type=F32)
        hu = jnp.dot(xb, wu_ref[...].astype(BF16), preferred_element_type=F32)
        h = (hg * jax.nn.sigmoid(hg)) * hu
        return jnp.dot(h.astype(BF16), wd_ref[...].astype(BF16), preferred_element_type=F32)

    last = pl.num_programs(1) - 1
    tm = x_ref.shape[0]

    @pl.when(f < last)
    def _():
        acc_ref[...] += ffn(pl.ds(0, tm))

    @pl.when(f == last)
    def _():
        half = tm // 2
        for rows in (pl.ds(0, half), pl.ds(half, half)):
            y = acc_ref[rows, :] + ffn(rows)
            r = _layer_norm(alpha * x_ref[rows, :] + y, g_ref[...], b_ref[...])
            out_ref[rows, :] = r
            outb_ref[rows, :] = r.astype(BF16)


def _swiglu(x, xb, wg, wu, wd, wl, prm, l, tm, tf, alpha, emit_weights):
    m = x.shape[0]
    vec = pl.BlockSpec((None, 1, D_MODEL), lambda i, f: (l, 0, 0))
    row = pl.BlockSpec((tm, D_MODEL), lambda i, f: (i, 0))
    col_tile = lambda layer: pl.BlockSpec((None, D_MODEL, tf), lambda i, f: (layer, 0, f))
    row_tile = lambda layer: pl.BlockSpec((None, tf, D_MODEL), lambda i, f: (layer, f, 0))
    out_specs = [row, row]
    out_shape = [jax.ShapeDtypeStruct((m, D_MODEL), F32), jax.ShapeDtypeStruct((m, D_MODEL), BF16)]
    if emit_weights:
        out_specs += [col_tile(0), col_tile(0), row_tile(0)]
        out_shape += [jax.ShapeDtypeStruct((1, D_MODEL, D_FF), BF16)] * 2
        out_shape += [jax.ShapeDtypeStruct((1, D_FF, D_MODEL), BF16)]
    return pl.pallas_call(
        functools.partial(_swiglu_kernel, alpha=alpha),
        grid=(m // tm, D_FF // tf),
        in_specs=[row, row, col_tile(wl), col_tile(wl), row_tile(wl), vec, vec],
        out_specs=out_specs,
        out_shape=out_shape,
        scratch_shapes=[pltpu.VMEM((tm, D_MODEL), F32)],
        compiler_params=_cparams(("arbitrary", "arbitrary")),
        name="swiglu",
    )(x, xb, wg, wu, wd, prm["ln2_g"], prm["ln2_b"])


def _sample_mix_kernel(za_ref, cbuf_ref, h0_ref, cw_ref, cb_ref, wa_ref, ba_ref, wx_ref, bx_ref, lam_ref,
                       vng_ref, vnb_ref, w4_ref, b4_ref,
                       ya_ref, yb_ref, hl_ref, cout_ref, vn_ref):
    nt = vn_ref.shape[0]
    nb = h0_ref.shape[0]
    zc = 4 * MIX_W // LANES
    gc = MIX_W // LANES

    def rows(t, grp):
        return jnp.concatenate(
            [za_ref[pl.ds(t * zc + grp * gc + c, nb, stride=nt * zc), :] for c in range(gc)], axis=-1)

    def put_rows(ref, t, val):
        for c in range(gc):
            ref[pl.ds(t * gc + c, nb, stride=nt * gc), :] = val[:, c * LANES:(c + 1) * LANES]

    xa = [rows(t, 0) for t in range(nt)]
    xp = [cbuf_ref[j] for j in range(CONV_W - 1)] + xa
    cw = cw_ref[...]
    xc = []
    for t in range(nt):
        acc = cb_ref[...] + cw[0:1] * xp[t]
        for j in range(1, CONV_W):
            acc = acc + cw[j:j + 1] * xp[t + j]
        xc.append(acc)
    a, b = _rglru_coeffs(jnp.concatenate(xc, axis=0), wa_ref, ba_ref, wx_ref, bx_ref, lam_ref)
    h = h0_ref[...]
    for t in range(nt):
        h = a[t * nb:(t + 1) * nb] * h + b[t * nb:(t + 1) * nb]
        put_rows(ya_ref, t, h * jax.nn.gelu(rows(t, 1)))
    hl_ref[...] = h
    for j in range(CONV_W - 1):
        cout_ref[j] = xp[nt + j]

    vn = []
    for t in range(nt):
        v = _layer_norm(jax.nn.gelu(rows(t, 3)), vng_ref[...], vnb_ref[...])
        vn_ref[t] = v
        vn.append(v)
    for t in range(nt):
        mix = b4_ref[t:t + 1]
        for s in range(t + 1):
            mix = mix + w4_ref[t, s:s + 1] * vn[s]
        put_rows(yb_ref, t, jax.nn.gelu(rows(t, 2)) * mix)


def _sample_mix(za, cbuf, h0, prm, l, nb, nt):
    m = nb * nt
    full = lambda shape: pl.BlockSpec(shape, lambda i: (0,) * len(shape))
    lsel = lambda shape: pl.BlockSpec((None,) + shape, lambda i: (l,) + (0,) * len(shape))
    return pl.pallas_call(
        _sample_mix_kernel,
        grid=(1,),
        in_specs=[
            full((m * 4 * MIX_W // LANES, LANES)),
            lsel((CONV_W - 1, nb, W_A)),
            lsel((nb, W_A)),
            lsel((CONV_W, W_A)), lsel((1, W_A)),
            lsel((H_A, BW_A, BW_A)), lsel((1, W_A)),
            lsel((H_A, BW_A, BW_A)), lsel((1, W_A)),
            lsel((1, W_A)),
            lsel((1, W_B)), lsel((1, W_B)),
            lsel((nt, nt, W_B)), lsel((nt, W_B)),
        ],
        out_specs=[
            full((m * W_A // LANES, LANES)), full((m * W_B // LANES, LANES)), full((nb, W_A)),
            full((CONV_W - 1, nb, W_A)), full((nt, nb, W_B)),
        ],
        out_shape=[
            jax.ShapeDtypeStruct((m * W_A // LANES, LANES), F32),
            jax.ShapeDtypeStruct((m * W_B // LANES, LANES), F32),
            jax.ShapeDtypeStruct((nb, W_A), F32),
            jax.ShapeDtypeStruct((CONV_W - 1, nb, W_A), F32),
            jax.ShapeDtypeStruct((nt, nb, W_B), F32),
        ],
        compiler_params=_cparams(("arbitrary",)),
        name="sample_mix",
    )(za, cbuf, h0, prm["conv_w"], prm["conv_b"], prm["w_a"], prm["b_a"], prm["w_x"], prm["b_x"],
      prm["lru_lam"], prm["vn_g"], prm["vn_b"], prm["w4"], prm["b4"])


def _expand_queries(q):
    hc = lax.broadcasted_iota(jnp.int32, (2 * H_C, W_C), 0)
    blk = lax.broadcasted_iota(jnp.int32, (2 * H_C, W_C), 1) // HD_C
    keep = hc == blk
    return jnp.concatenate(
        [jnp.where(keep, jnp.broadcast_to(q[t:t + 1], (2 * H_C, W_C)), 0.0) for t in range(q.shape[0])],
        axis=0)


def _page_keys(ref, page):
    return jnp.concatenate(
        [ref[pl.ds(_k_piece(hc), page, stride=HEAD_ROWS), :] for hc in range(2 * H_C)], axis=-1)


def _page_values(ref, page):
    return jnp.concatenate(
        [ref[pl.ds(_v_piece(h, half), page, stride=HEAD_ROWS), :] for h in range(H_C) for half in range(2)],
        axis=-1)


def _paged_attn_kernel(pt_ref, q_ref, kn_ref, vn_ref, lq1_ref, lk1_ref, lq2_ref, lk2_ref, g_ref, *rest,
                       pages_per_step, lambda_init):
    del pt_ref
    k_refs = rest[:pages_per_step]
    v_refs = rest[pages_per_step:2 * pages_per_step]
    o_ref, qe_ref, m_ref, l_ref, acc_ref = rest[2 * pages_per_step:]
    j = pl.program_id(1)
    nt = q_ref.shape[0]
    page = k_refs[0].shape[0] // HEAD_ROWS

    @pl.when(j == 0)
    def _():
        qe_ref[...] = _expand_queries(q_ref[...]).astype(BF16)
        m_ref[...] = jnp.full(m_ref.shape, NEG_INF, F32)
        l_ref[...] = jnp.zeros(l_ref.shape, F32)
        acc_ref[...] = jnp.zeros(acc_ref.shape, F32)

    qe = qe_ref[...]
    s = jnp.concatenate(
        [lax.dot_general(qe, _page_keys(k_refs[p], page).astype(BF16), (((1,), (1,)), ((), ())),
                         preferred_element_type=F32) for p in range(pages_per_step)], axis=-1)
    m_prev = m_ref[...]
    m_new = jnp.maximum(m_prev, jnp.max(s, axis=-1, keepdims=True))
    alpha = jnp.exp2(m_prev - m_new)
    p = jnp.exp2(s - m_new)
    l_ref[...] = alpha * l_ref[...] + jnp.sum(p, axis=-1, keepdims=True)
    pb = p.astype(BF16)
    pv = jnp.dot(pb[:, 0:page], _page_values(v_refs[0], page).astype(BF16), preferred_element_type=F32)
    for i in range(1, pages_per_step):
        pv = pv + jnp.dot(pb[:, i * page:(i + 1) * page], _page_values(v_refs[i], page).astype(BF16),
                          preferred_element_type=F32)
    acc_ref[...] = alpha * acc_ref[...] + pv
    m_ref[...] = m_new

    @pl.when(j == pl.num_programs(1) - 1)
    def _():
        qe32 = _expand_queries(q_ref[...])
        trow = lax.broadcasted_iota(jnp.int32, (nt * 2 * H_C, 1), 0) // (2 * H_C)
        kn = kn_ref[...]
        vn = vn_ref[...]
        sc = [jnp.where(trow >= t, jnp.sum(qe32 * kn[t:t + 1], axis=-1, keepdims=True), NEG_INF)
              for t in range(nt)]
        m_prev = m_ref[...]
        m_new = m_prev
        for t in range(nt):
            m_new = jnp.maximum(m_new, sc[t])
        alpha = jnp.exp2(m_prev - m_new)
        lsum = alpha * l_ref[...]
        acc = alpha * acc_ref[...]
        for t in range(nt):
            pt = jnp.exp2(sc[t] - m_new)
            lsum = lsum + pt
            acc = acc + pt * vn[t:t + 1]
        accn = acc / lsum
        lam = _diff_lambda(lq1_ref[...], lk1_ref[...], lq2_ref[...], lk2_ref[...], lambda_init)
        hc = lax.broadcasted_iota(jnp.int32, (nt * 2 * H_C, 1), 0) % (2 * H_C)
        outs = []
        for h in range(H_C):
            w = jnp.where(hc == 2 * h, 1.0, jnp.where(hc == 2 * h + 1, -lam, 0.0))
            o = jnp.sum((accn[:, h * DV_C:(h + 1) * DV_C] * w).reshape(nt, 2 * H_C, DV_C), axis=1)
            ms = jnp.mean(o * o, axis=-1, keepdims=True)
            outs.append(o * lax.rsqrt(ms + LN_EPS) * g_ref[...] * (1.0 - lambda_init))
        o_ref[...] = jnp.concatenate(outs, axis=-1)


def _paged_attn(page_table, qkv3, cache_k, cache_v, prm, l, pages_per_step, lambda_init):
    nb, nt, _ = qkv3.shape
    n_pages = page_table.shape[0] // nb
    page_rows = cache_k.shape[2]
    lamspec = pl.BlockSpec((None, 1, HD_C), lambda b, j, pt: (l, 0, 0))

    def page_spec(p):
        return pl.BlockSpec((None, None, page_rows, LANES),
                            lambda b, j, pt: (l, pt[b * n_pages + j * pages_per_step + p], 0, 0))

    grid_spec = pltpu.PrefetchScalarGridSpec(
        num_scalar_prefetch=1,
        grid=(nb, n_pages // pages_per_step),
        in_specs=[
            pl.BlockSpec((None, nt, W_C), lambda b, j, pt: (b, 0, 0)),
            pl.BlockSpec((None, nt, W_C), lambda b, j, pt: (b, 0, 1)),
            pl.BlockSpec((None, nt, W_C), lambda b, j, pt: (b, 0, 2)),
            lamspec, lamspec, lamspec, lamspec,
            pl.BlockSpec((None, 1, DV_C), lambda b, j, pt: (l, 0, 0)),
        ] + [page_spec(p) for p in range(pages_per_step)] * 2,
        out_specs=pl.BlockSpec((None, nt, W_C), lambda b, j, pt: (b, 0, 0)),
        scratch_shapes=[
            pltpu.VMEM((nt * 2 * H_C, W_C), BF16),
            pltpu.VMEM((nt * 2 * H_C, 1), F32),
            pltpu.VMEM((nt * 2 * H_C, 1), F32),
            pltpu.VMEM((nt * 2 * H_C, W_C), F32),
        ],
    )
    return pl.pallas_call(
        functools.partial(_paged_attn_kernel, pages_per_step=pages_per_step, lambda_init=lambda_init),
        grid_spec=grid_spec,
        out_shape=jax.ShapeDtypeStruct((nb, nt, W_C), F32),
        compiler_params=_cparams(("arbitrary", "arbitrary")),
        name="paged_attn",
    )(page_table, qkv3, qkv3, qkv3, prm["lam_q1"], prm["lam_k1"], prm["lam_q2"], prm["lam_k2"],
      prm["subln_g"], *([cache_k] * pages_per_step), *([cache_v] * pages_per_step))


def _values_to_head_rows(v):
    lead = v.shape[:-3]
    tokens = v.shape[-3]
    v = v.reshape(lead + (tokens, H_C, 2, LANES))
    v = jnp.swapaxes(v, -3, -2)
    return v.reshape(lead + (tokens * HEAD_ROWS, LANES))


def _head_rows_to_values(r, lead):
    depth = r.shape[0]
    v = r.reshape((depth,) + lead + (2, H_C, LANES))
    v = jnp.swapaxes(v, -3, -2)
    return v.reshape((depth,) + lead + (H_C, DV_C))


def _largest_tile(n, cap, mult):
    t = min(n, cap)
    while n % t or t % mult:
        t -= 1
    return t


def kernel(x_prompt, x_sample, cache_k, cache_v, state_lru_h, state_conv, page_table, w_in, w_out, conv_w,
           conv_b, w_a, b_a, w_x, b_x, lru_lam, vn_g, vn_b, w_s, b_s, lam_q1, lam_k1, lam_q2, lam_k2,
           subln_g, ln1_g, ln1_b, ln2_g, ln2_b, w_gate, w_up, w_down):
    depth = w_in.shape[0]
    nbp, seq, _ = x_prompt.shape
    nbs, nts, _ = x_sample.shape
    n_pool, page = cache_k.shape[1], cache_k.shape[2]
    n_pages = page_table.shape[1]
    alpha = (2.0 * depth) ** 0.25
    mp, ms = nbp * seq, nbs * nts

    row3 = lambda a: a.reshape(depth, 1, a.shape[-1])
    prm = {
        "w_a": w_a.astype(BF16), "w_x": w_x.astype(BF16),
        "conv_w": conv_w, "conv_b": row3(conv_b), "b_a": row3(b_a), "b_x": row3(b_x),
        "lru_lam": row3(lru_lam), "vn_g": row3(vn_g), "vn_b": row3(vn_b), "w_s": w_s,
        "bs_rows": jnp.repeat(jnp.transpose(b_s, (0, 2, 1)), GW_B, axis=-1),
        "w4": jnp.repeat(jnp.transpose(w_s[:, :, :nts, :nts], (0, 2, 3, 1)), GW_B, axis=-1),
        "b4": jnp.repeat(jnp.transpose(b_s[:, :, :nts], (0, 2, 1)), GW_B, axis=-1),
        "lam_q1": row3(lam_q1), "lam_k1": row3(lam_k1), "lam_q2": row3(lam_q2), "lam_k2": row3(lam_k2),
        "subln_g": row3(subln_g), "ln1_g": row3(ln1_g), "ln1_b": row3(ln1_b),
        "ln2_g": row3(ln2_g), "ln2_b": row3(ln2_b),
    }
    ck = cache_k.reshape(depth, n_pool, page * HEAD_ROWS, LANES)
    cv = _values_to_head_rows(cache_v)
    pt_flat = page_table.reshape(-1)
    sconv = jnp.transpose(state_conv, (0, 2, 1, 3))
    zero_cbuf = jnp.zeros((nbp, SUBLANES, W_A), F32)
    zero_h = jnp.zeros((nbp, 1, W_A), F32)

    tm_in = _largest_tile(mp, 1024, CHUNK)
    tm_p = _largest_tile(mp, 512, CHUNK)
    tt_a = _largest_tile(seq, 256, SUBLANES)
    tk = _largest_tile(seq, 512, CHUNK)
    tq = _largest_tile(tk, 256, CHUNK)
    tf = _largest_tile(D_FF, 512, 128)
    pps = _largest_tile(n_pages, 16, 1)

    xp = x_prompt.reshape(mp, D_MODEL)
    xs = x_sample.reshape(ms, D_MODEL)
    xp_b, xs_b = xp.astype(BF16), xs.astype(BF16)
    kbuf_p = jnp.zeros((depth, mp * HEAD_ROWS, LANES), F32)
    vbuf_p = jnp.zeros((depth, mp * HEAD_ROWS, LANES), F32)
    kbuf_s = jnp.zeros((depth, ms * HEAD_ROWS, LANES), F32)
    vbuf_s = jnp.zeros((depth, ms * HEAD_ROWS, LANES), F32)
    outs = [[] for _ in range(5)]
    for l in range(depth):
        lambda_init = 0.8 - 0.6 * math.exp(-0.3 * l)
        za_s, qkv_s, kbuf_s, vbuf_s, w_in_b = _inproj(xs_b, w_in, l, kbuf_s, vbuf_s, l, ms, F32, True)
        ya_s, yb_s, h_s, c_s, vn_s = _sample_mix(za_s.reshape(-1, LANES), sconv, state_lru_h, prm, l, nbs,
                                                  nts)
        ya_s, yb_s = ya_s.reshape(ms, W_A), yb_s.reshape(ms, W_B)
        o_s = _paged_attn(pt_flat, qkv_s.reshape(nbs, nts, 3 * W_C), ck, cv, prm, l, pps, lambda_init)
        xs, xs_b, w_out_b = _outproj(ya_s, yb_s, o_s.reshape(ms, W_C), xs, w_out, l, prm, l, ms, alpha, True)
        xs, xs_b, wg_b, wu_b, wd_b = _swiglu(xs, xs_b, w_gate, w_up, w_down, l, prm, l, ms, tf, alpha, True)
        za, qkv, kbuf_p, vbuf_p = _inproj(xp_b, w_in_b, 0, kbuf_p, vbuf_p, l, tm_in, BF16, False)
        ya, h_p, c_p = _mix_a(za, zero_cbuf, zero_h, prm, l, nbp, tt_a)
        yb = _mix_b(za, prm, l, tm_p)
        o = _attn(qkv, prm, l, nbp, seq, tq, tk, lambda_init)
        xp, xp_b = _outproj(ya, yb, o, xp, w_out_b, 0, prm, l, tm_p, alpha, False)
        xp, xp_b = _swiglu(xp, xp_b, wg_b, wu_b, wd_b, 0, prm, l, tm_p, tf, alpha, False)

        for lst, val in zip(outs, (h_p, h_s, c_p, c_s, vn_s)):
            lst.append(val)

    hp, hs, cp, cs, chv = [jnp.stack(v) for v in outs]
    return (
        xp.reshape(nbp, seq, D_MODEL),
        xs.reshape(nbs, nts, D_MODEL),
        kbuf_p.reshape(depth, nbp, seq, H_C, 2, HD_C),
        _head_rows_to_values(vbuf_p, (nbp, seq)),
        kbuf_s.reshape(depth, nbs, nts, H_C, 2, HD_C),
        _head_rows_to_values(vbuf_s, (nbs, nts)),
        hp.reshape(depth, nbp, W_A),
        hs,
        cp[:, :, SUBLANES - (CONV_W - 1):, :],
        jnp.transpose(cs, (0, 2, 1, 3)),
        jnp.transpose(chv, (0, 2, 1, 3)),
    )
```

```python
import functools
import math

import jax
import jax.numpy as jnp
from jax import lax
from jax.experimental import pallas as pl
from jax.experimental.pallas import tpu as pltpu

F32 = jnp.float32
BF16 = jnp.bfloat16

D_MODEL = 2048
W_A = 512
H_A = 4
BW_A = 128
CONV_W = 4
RG_C = 8.0
W_B = 512
G_B = 4
GW_B = 128
CHUNK = 128
W_C = 1024
H_C = 4
HD_C = 128
DV_C = 256
W_IN = 2 * W_A + 2 * W_B + 3 * W_C
D_FF = 5632
LN_EPS = 1e-5
NEG_INF = -1e30
QK_SCALE = HD_C ** -0.5 * math.log2(math.e)

SUBLANES = 8
LANES = 128
COL_TILE = 1024
MIX_W = 512
HEAD_ROWS = W_C // LANES
VMEM_LIMIT = 56 * 1024 * 1024


def _cparams(sem):
    return pltpu.CompilerParams(dimension_semantics=sem, vmem_limit_bytes=VMEM_LIMIT)


def _layer_norm(r, g, b):
    mu = jnp.mean(r, axis=-1, keepdims=True)
    d = r - mu
    var = jnp.mean(d * d, axis=-1, keepdims=True)
    return d * lax.rsqrt(var + LN_EPS) * g + b


def _softplus(x):
    return jnp.maximum(x, 0.0) + jnp.log1p(jnp.exp(-jnp.abs(x)))


def _neg_expm1_nonpos(x):
    t = jnp.tanh(0.5 * x)
    return -2.0 * t / (1.0 - t)


def _diff_lambda(lq1, lk1, lq2, lk2, lambda_init):
    s1 = jnp.sum(lq1 * lk1, axis=-1, keepdims=True)
    s2 = jnp.sum(lq2 * lk2, axis=-1, keepdims=True)
    return jnp.exp(s1) - jnp.exp(s2) + lambda_init


def _lane_tile(x, width):
    return jnp.concatenate([x] * (width // LANES), axis=-1)


def _k_piece(hc):
    return hc


def _v_piece(h, half):
    return half * H_C + h


def _inproj_kernel(x_ref, w_ref, kin_ref, vin_ref, za_ref, qkv_ref, kf_ref, vf_ref, *wb_refs):
    del kin_ref, vin_ref
    j = pl.program_id(1)
    tm = x_ref.shape[0]
    for wb_ref in wb_refs:
        wb_ref[...] = w_ref[...].astype(BF16)

    def project():
        return jnp.dot(x_ref[...], w_ref[...].astype(BF16), preferred_element_type=F32)

    @pl.when(j < 2)
    def _():
        za_ref[...] = project()

    @pl.when(j == 2)
    def _():
        qkv_ref[...] = (project() * QK_SCALE).astype(qkv_ref.dtype)

    @pl.when(j == 3)
    def _():
        z = project()
        qkv_ref[...] = z.astype(qkv_ref.dtype)
        for hc in range(2 * H_C):
            kf_ref[pl.ds(_k_piece(hc), tm, stride=HEAD_ROWS), :] = z[:, hc * HD_C:(hc + 1) * HD_C]

    @pl.when(j == 4)
    def _():
        z = project()
        qkv_ref[...] = z.astype(qkv_ref.dtype)
        for h in range(H_C):
            for half in range(2):
                c0 = h * DV_C + half * LANES
                vf_ref[pl.ds(_v_piece(h, half), tm, stride=HEAD_ROWS), :] = z[:, c0:c0 + LANES]


def _inproj(xb, w, wl, kbuf, vbuf, l, tm, qkv_dtype, emit_weights):
    m = xb.shape[0]
    nj = W_IN // COL_TILE
    kv_spec = pl.BlockSpec((None, tm * HEAD_ROWS, LANES), lambda i, j: (l, i, 0))
    out_specs = [
        pl.BlockSpec((tm, COL_TILE), lambda i, j: (i, jnp.minimum(j, 1))),
        pl.BlockSpec((tm, COL_TILE), lambda i, j: (i, jnp.clip(j - 2, 0, 2))),
        kv_spec, kv_spec,
    ]
    out_shape = [
        jax.ShapeDtypeStruct((m, 4 * MIX_W), F32),
        jax.ShapeDtypeStruct((m, 3 * W_C), qkv_dtype),
        jax.ShapeDtypeStruct(kbuf.shape, F32),
        jax.ShapeDtypeStruct(vbuf.shape, F32),
    ]
    if emit_weights:
        out_specs.append(pl.BlockSpec((None, D_MODEL, COL_TILE), lambda i, j: (0, 0, j)))
        out_shape.append(jax.ShapeDtypeStruct((1, D_MODEL, W_IN), BF16))
    return pl.pallas_call(
        _inproj_kernel,
        grid=(m // tm, nj),
        in_specs=[
            pl.BlockSpec((tm, D_MODEL), lambda i, j: (i, 0)),
            pl.BlockSpec((None, D_MODEL, COL_TILE), lambda i, j: (wl, 0, j)),
            pl.BlockSpec(memory_space=pl.ANY),
            pl.BlockSpec(memory_space=pl.ANY),
        ],
        out_specs=out_specs,
        out_shape=out_shape,
        input_output_aliases={2: 2, 3: 3},
        compiler_params=_cparams(("arbitrary", "arbitrary")),
        name="inproj",
    )(xb, w, kbuf, vbuf)


def _shift_rows(x, s, fill):
    n, w = x.shape
    if s % SUBLANES == 0:
        return jnp.concatenate([jnp.full((s, w), fill, x.dtype), x[:n - s]], axis=0)
    row = lax.broadcasted_iota(jnp.int32, x.shape, 0)
    return jnp.where(row < s, fill, pltpu.roll(x, s, 0))


def _block_gate(xcb, w_ref, b_ref):
    parts = [jnp.dot(xcb[:, h * BW_A:(h + 1) * BW_A], w_ref[h], preferred_element_type=F32)
             for h in range(H_A)]
    return jax.nn.sigmoid(jnp.concatenate(parts, axis=-1) + b_ref[...])


def _rglru_coeffs(xc, wa_ref, ba_ref, wx_ref, bx_ref, lam_ref):
    xcb = xc.astype(BF16)
    r = _block_gate(xcb, wa_ref, ba_ref)
    gi = _block_gate(xcb, wx_ref, bx_ref)
    log_a = -RG_C * r * _softplus(-lam_ref[...])
    a = jnp.exp(log_a)
    b = jnp.sqrt(_neg_expm1_nonpos(2.0 * log_a)) * (gi * xc)
    return a, b


def _mix_a_kernel(xa_ref, ga_ref, cbuf_ref, h0_ref, cw_ref, cb_ref, wa_ref, ba_ref, wx_ref, bx_ref,
                  lam_ref, ya_ref, hl_ref, cout_ref, tail_ref, hc_ref):
    t = pl.program_id(1)
    tt = xa_ref.shape[0]

    @pl.when(t == 0)
    def _():
        tail_ref[...] = cbuf_ref[...]
        hc_ref[...] = h0_ref[...]

    xa = xa_ref[...]
    tail = tail_ref[...]
    row8 = lax.broadcasted_iota(jnp.int32, tail.shape, 0)

    def delayed(k):
        r = pltpu.roll(xa, k, 0)
        first = jnp.where(row8 < k, pltpu.roll(tail, k, 0), r[0:SUBLANES])
        return jnp.concatenate([first, r[SUBLANES:]], axis=0)

    cw = cw_ref[...]
    xc = cb_ref[...] + cw[3:4] * xa
    for k in range(1, CONV_W):
        xc = xc + cw[CONV_W - 1 - k:CONV_W - k] * delayed(k)
    tail_ref[...] = xa[tt - SUBLANES:tt]

    a, b = _rglru_coeffs(xc, wa_ref, ba_ref, wx_ref, bx_ref, lam_ref)
    s = 1
    while s < tt:
        b = b + a * _shift_rows(b, s, 0.0)
        a = a * _shift_rows(a, s, 1.0)
        s *= 2
    h = a * hc_ref[...] + b
    hc_ref[...] = h[tt - 1:tt]
    ya_ref[...] = (h * jax.nn.gelu(ga_ref[...])).astype(ya_ref.dtype)

    @pl.when(t == pl.num_programs(1) - 1)
    def _():
        hl_ref[...] = h[tt - 1:tt]
        cout_ref[...] = xa[tt - SUBLANES:tt]


def _mix_a(za, cbuf8, h0, prm, l, nb, tt):
    m = za.shape[0]
    nt = m // nb // tt
    vec = pl.BlockSpec((None, 1, W_A), lambda b, t: (l, 0, 0))
    blk = pl.BlockSpec((None, H_A, BW_A, BW_A), lambda b, t: (l, 0, 0, 0))
    return pl.pallas_call(
        _mix_a_kernel,
        grid=(nb, nt),
        in_specs=[
            pl.BlockSpec((tt, W_A), lambda b, t: (b * nt + t, 0)),
            pl.BlockSpec((tt, W_A), lambda b, t: (b * nt + t, 1)),
            pl.BlockSpec((None, SUBLANES, W_A), lambda b, t: (b, 0, 0)),
            pl.BlockSpec((None, 1, W_A), lambda b, t: (b, 0, 0)),
            pl.BlockSpec((None, CONV_W, W_A), lambda b, t: (l, 0, 0)),
            vec, blk, vec, blk, vec, vec,
        ],
        out_specs=[
            pl.BlockSpec((tt, W_A), lambda b, t: (b * nt + t, 0)),
            pl.BlockSpec((None, 1, W_A), lambda b, t: (b, 0, 0)),
            pl.BlockSpec((None, SUBLANES, W_A), lambda b, t: (b, 0, 0)),
        ],
        out_shape=[
            jax.ShapeDtypeStruct((m, W_A), BF16),
            jax.ShapeDtypeStruct((nb, 1, W_A), F32),
            jax.ShapeDtypeStruct((nb, SUBLANES, W_A), F32),
        ],
        scratch_shapes=[pltpu.VMEM((SUBLANES, W_A), F32), pltpu.VMEM((1, W_A), F32)],
        compiler_params=_cparams(("arbitrary", "arbitrary")),
        name="mix_a",
    )(za, za, cbuf8, h0, prm["conv_w"], prm["conv_b"], prm["w_a"], prm["b_a"], prm["w_x"], prm["b_x"],
      prm["lru_lam"])


def _mix_b_kernel(ub_ref, vb_ref, ws_ref, bs_ref, g_ref, b_ref, yb_ref):
    tt = ub_ref.shape[0]
    vn = _layer_norm(jax.nn.gelu(vb_ref[...]), g_ref[...], b_ref[...]).astype(BF16)
    row = lax.broadcasted_iota(jnp.int32, (CHUNK, CHUNK), 0)
    col = lax.broadcasted_iota(jnp.int32, (CHUNK, CHUNK), 1)
    wt = [jnp.where(col <= row, ws_ref[g], 0.0).astype(BF16) for g in range(G_B)]
    bias = bs_ref[...]
    chunks = []
    for c in range(tt // CHUNK):
        vc = vn[c * CHUNK:(c + 1) * CHUNK]
        parts = [jnp.dot(wt[g], vc[:, g * GW_B:(g + 1) * GW_B], preferred_element_type=F32)
                 for g in range(G_B)]
        chunks.append(jnp.concatenate(parts, axis=-1) + bias)
    mix = jnp.concatenate(chunks, axis=0)
    yb_ref[...] = (jax.nn.gelu(ub_ref[...]) * mix).astype(yb_ref.dtype)


def _mix_b(za, prm, l, tt):
    m = za.shape[0]
    return pl.pallas_call(
        _mix_b_kernel,
        grid=(m // tt,),
        in_specs=[
            pl.BlockSpec((tt, W_B), lambda i: (i, 2)),
            pl.BlockSpec((tt, W_B), lambda i: (i, 3)),
            pl.BlockSpec((None, G_B, CHUNK, CHUNK), lambda i: (l, 0, 0, 0)),
            pl.BlockSpec((None, CHUNK, W_B), lambda i: (l, 0, 0)),
            pl.BlockSpec((None, 1, W_B), lambda i: (l, 0, 0)),
            pl.BlockSpec((None, 1, W_B), lambda i: (l, 0, 0)),
        ],
        out_specs=pl.BlockSpec((tt, W_B), lambda i: (i, 0)),
        out_shape=jax.ShapeDtypeStruct((m, W_B), BF16),
        compiler_params=_cparams(("arbitrary",)),
        name="mix_b",
    )(za, za, prm["w_s"], prm["bs_rows"], prm["vn_g"], prm["vn_b"])


def _attn_kernel(q_ref, k_ref, v_ref, lq1_ref, lk1_ref, lq2_ref, lk2_ref, g_ref, o_ref,
                 m_ref, l_ref, acc_ref, p_ref, a_ref, *, lambda_init, tk):
    qi = pl.program_id(2)
    tq = q_ref.shape[0]
    m_ref[...] = jnp.full(m_ref.shape, NEG_INF, F32)
    l_ref[...] = jnp.zeros(l_ref.shape, F32)
    acc_ref[...] = jnp.zeros(acc_ref.shape, F32)
    q = q_ref[...]

    def scores(kstart, slot, masked):
        k = k_ref[pl.ds(kstart, tk), :]
        if masked:
            row = lax.broadcasted_iota(jnp.int32, (tq, tk), 0) + qi * tq
            col = lax.broadcasted_iota(jnp.int32, (tq, tk), 1) + kstart
            visible = col <= row
        for c in range(2):
            s = lax.dot_general(q[:, c * HD_C:(c + 1) * HD_C], k[:, c * HD_C:(c + 1) * HD_C],
                                (((1,), (1,)), ((), ())), preferred_element_type=F32)
            if masked:
                s = jnp.where(visible, s, NEG_INF)
            m_prev = m_ref[c]
            m_new = jnp.maximum(m_prev, jnp.max(s, axis=-1, keepdims=True))
            alpha = jnp.exp2(m_prev - m_new)
            p = jnp.exp2(s - _lane_tile(m_new, tk))
            l_ref[c] = alpha * l_ref[c] + jnp.sum(p, axis=-1, keepdims=True)
            m_ref[c] = m_new
            p_ref[slot, c] = p.astype(BF16)
            a_ref[slot, c] = alpha

    def accumulate(vstart, slot):
        v = v_ref[pl.ds(vstart, tk), :]
        for c in range(2):
            acc_ref[c] = _lane_tile(a_ref[slot, c], DV_C) * acc_ref[c] + jnp.dot(
                p_ref[slot, c], v, preferred_element_type=F32)

    n_full = (qi * tq) // tk
    diag = pl.multiple_of(n_full * tk, tk)
    scores(diag, 0, True)

    def body(i, pending):
        vstart, slot = pending
        kstart = pl.multiple_of(i * tk, tk)
        accumulate(pl.multiple_of(vstart, tk), slot)
        scores(kstart, 1 - slot, False)
        return kstart, 1 - slot

    def pair_body(i, pending):
        return body(2 * i + 1, body(2 * i, pending))

    n_pairs = n_full // 2
    pending = lax.fori_loop(0, n_pairs, pair_body, (diag, jnp.int32(0)))
    vstart, slot = lax.fori_loop(2 * n_pairs, n_full, body, pending)
    accumulate(pl.multiple_of(vstart, tk), slot)

    lam = _diff_lambda(lq1_ref[...], lk1_ref[...], lq2_ref[...], lk2_ref[...], lambda_init)
    o = acc_ref[0] / _lane_tile(l_ref[0], DV_C) - lam * (acc_ref[1] / _lane_tile(l_ref[1], DV_C))
    ms = jnp.mean(o * o, axis=-1, keepdims=True)
    o_ref[...] = (o * lax.rsqrt(ms + LN_EPS) * g_ref[...] * (1.0 - lambda_init)).astype(o_ref.dtype)


def _attn(qkv, prm, l, nb, seq, tq, tk, lambda_init):
    m = qkv.shape[0]
    nq = seq // tq
    lamspec = pl.BlockSpec((None, 1, HD_C), lambda b, h, qi: (l, 0, 0))
    return pl.pallas_call(
        functools.partial(_attn_kernel, lambda_init=lambda_init, tk=tk),
        grid=(nb, H_C, nq),
        in_specs=[
            pl.BlockSpec((tq, DV_C), lambda b, h, qi: (b * nq + qi, h)),
            pl.BlockSpec((seq, DV_C), lambda b, h, qi: (b, H_C + h)),
            pl.BlockSpec((seq, DV_C), lambda b, h, qi: (b, 2 * H_C + h)),
            lamspec, lamspec, lamspec, lamspec,
            pl.BlockSpec((None, 1, DV_C), lambda b, h, qi: (l, 0, 0)),
        ],
        out_specs=pl.BlockSpec((tq, DV_C), lambda b, h, qi: (b * nq + qi, h)),
        out_shape=jax.ShapeDtypeStruct((m, W_C), BF16),
        scratch_shapes=[
            pltpu.VMEM((2, tq, LANES), F32),
            pltpu.VMEM((2, tq, LANES), F32),
            pltpu.VMEM((2, tq, DV_C), F32),
            pltpu.VMEM((2, 2, tq, tk), BF16),
            pltpu.VMEM((2, 2, tq, LANES), F32),
        ],
        compiler_params=_cparams(("arbitrary",) * 3),
        name="attn",
    )(qkv, qkv, qkv, prm["lam_q1"], prm["lam_k1"], prm["lam_q2"], prm["lam_k2"], prm["subln_g"])


def _outproj_kernel(ya_ref, yb_ref, o_ref, x_ref, w_ref, g_ref, b_ref, out_ref, outb_ref, *wb_refs, alpha):
    w = w_ref[...].astype(BF16)
    for wb_ref in wb_refs:
        wb_ref[...] = w
    half = x_ref.shape[0] // 2
    for rows in (pl.ds(0, half), pl.ds(half, half)):
        cat = jnp.concatenate([ya_ref[rows, :].astype(BF16), yb_ref[rows, :].astype(BF16),
                               o_ref[rows, :].astype(BF16)], axis=-1)
        y = jnp.dot(cat, w, preferred_element_type=F32)
        r = _layer_norm(alpha * x_ref[rows, :] + y, g_ref[...], b_ref[...])
        out_ref[rows, :] = r
        outb_ref[rows, :] = r.astype(BF16)


def _outproj(ya, yb, o, x, w, wl, prm, l, tm, alpha, emit_weights):
    m = x.shape[0]
    vec = pl.BlockSpec((None, 1, D_MODEL), lambda i: (l, 0, 0))
    row = pl.BlockSpec((tm, D_MODEL), lambda i: (i, 0))
    wspec = pl.BlockSpec((None, D_MODEL, D_MODEL), lambda i: (wl, 0, 0), pipeline_mode=pl.Buffered(1))
    out_specs = [row, row]
    out_shape = [jax.ShapeDtypeStruct((m, D_MODEL), F32), jax.ShapeDtypeStruct((m, D_MODEL), BF16)]
    if emit_weights:
        out_specs.append(pl.BlockSpec((None, D_MODEL, D_MODEL), lambda i: (0, 0, 0)))
        out_shape.append(jax.ShapeDtypeStruct((1, D_MODEL, D_MODEL), BF16))
    return pl.pallas_call(
        functools.partial(_outproj_kernel, alpha=alpha),
        grid=(m // tm,),
        in_specs=[
            pl.BlockSpec((tm, W_A), lambda i: (i, 0)),
            pl.BlockSpec((tm, W_B), lambda i: (i, 0)),
            pl.BlockSpec((tm, W_C), lambda i: (i, 0)),
            row,
            wspec,
            vec, vec,
        ],
        out_specs=out_specs,
        out_shape=out_shape,
        compiler_params=_cparams(("arbitrary",)),
        name="outproj",
    )(ya, yb, o, x, w, prm["ln1_g"], prm["ln1_b"])


def _swiglu_kernel(x_ref, xb_ref, wg_ref, wu_ref, wd_ref, g_ref, b_ref, out_ref, outb_ref, *rest, alpha):
    *wb_refs, acc_ref = rest
    f = pl.program_id(1)

    @pl.when(f == 0)
    def _():
        acc_ref[...] = jnp.zeros(acc_ref.shape, F32)

    for wb_ref, w_ref in zip(wb_refs, (wg_ref, wu_ref, wd_ref)):
        wb_ref[...] = w_ref[...].astype(BF16)

    def ffn(rows):
        xb = xb_ref[rows, :]
        hg = jnp.dot(xb, wg_ref[...].astype(BF16), preferred_element_type=F32)
        hu = jnp.dot(xb, wu_ref[...].astype(BF16), preferred_element_type=F32)
        h = (hg * jax.nn.sigmoid(hg)) * hu
        return jnp.dot(h.astype(BF16), wd_ref[...].astype(BF16), preferred_element_type=F32)

    last = pl.num_programs(1) - 1
    tm = x_ref.shape[0]

    @pl.when(f < last)
    def _():
        acc_ref[...] += ffn(pl.ds(0, tm))

    @pl.when(f == last)
    def _():
        half = tm // 2
        for rows in (pl.ds(0, half), pl.ds(half, half)):
            y = acc_ref[rows, :] + ffn(rows)
            r = _layer_norm(alpha * x_ref[rows, :] + y, g_ref[...], b_ref[...])
            out_ref[rows, :] = r
            outb_ref[rows, :] = r.astype(BF16)


def _swiglu(x, xb, wg, wu, wd, wl, prm, l, tm, tf, alpha, emit_weights):
    m = x.shape[0]
    vec = pl.BlockSpec((None, 1, D_MODEL), lambda i, f: (l, 0, 0))
    row = pl.BlockSpec((tm, D_MODEL), lambda i, f: (i, 0))
    col_tile = lambda layer: pl.BlockSpec((None, D_MODEL, tf), lambda i, f: (layer, 0, f))
    row_tile = lambda layer: pl.BlockSpec((None, tf, D_MODEL), lambda i, f: (layer, f, 0))
    out_specs = [row, row]
    out_shape = [jax.ShapeDtypeStruct((m, D_MODEL), F32), jax.ShapeDtypeStruct((m, D_MODEL), BF16)]
    if emit_weights:
        out_specs += [col_tile(0), col_tile(0), row_tile(0)]
        out_shape += [jax.ShapeDtypeStruct((1, D_MODEL, D_FF), BF16)] * 2
        out_shape += [jax.ShapeDtypeStruct((1, D_FF, D_MODEL), BF16)]
    return pl.pallas_call(
        functools.partial(_swiglu_kernel, alpha=alpha),
        grid=(m // tm, D_FF // tf),
        in_specs=[row, row, col_tile(wl), col_tile(wl), row_tile(wl), vec, vec],
        out_specs=out_specs,
        out_shape=out_shape,
        scratch_shapes=[pltpu.VMEM((tm, D_MODEL), F32)],
        compiler_params=_cparams(("arbitrary", "arbitrary")),
        name="swiglu",
    )(x, xb, wg, wu, wd, prm["ln2_g"], prm["ln2_b"])


def _sample_mix_kernel(za_ref, cbuf_ref, h0_ref, cw_ref, cb_ref, wa_ref, ba_ref, wx_ref, bx_ref, lam_ref,
                       vng_ref, vnb_ref, w4_ref, b4_ref,
                       ya_ref, yb_ref, hl_ref, cout_ref, vn_ref):
    nt = vn_ref.shape[0]
    nb = h0_ref.shape[0]
    zc = 4 * MIX_W // LANES
    gc = MIX_W // LANES

    def rows(t, grp):
        return jnp.concatenate(
            [za_ref[pl.ds(t * zc + grp * gc + c, nb, stride=nt * zc), :] for c in range(gc)], axis=-1)

    def put_rows(ref, t, val):
        for c in range(gc):
            ref[pl.ds(t * gc + c, nb, stride=nt * gc), :] = val[:, c * LANES:(c + 1) * LANES]

    xa = [rows(t, 0) for t in range(nt)]
    xp = [cbuf_ref[j] for j in range(CONV_W - 1)] + xa
    cw = cw_ref[...]
    xc = []
    for t in range(nt):
        acc = cb_ref[...] + cw[0:1] * xp[t]
        for j in range(1, CONV_W):
            acc = acc + cw[j:j + 1] * xp[t + j]
        xc.append(acc)
    a, b = _rglru_coeffs(jnp.concatenate(xc, axis=0), wa_ref, ba_ref, wx_ref, bx_ref, lam_ref)
    h = h0_ref[...]
    for t in range(nt):
        h = a[t * nb:(t + 1) * nb] * h + b[t * nb:(t + 1) * nb]
        put_rows(ya_ref, t, h * jax.nn.gelu(rows(t, 1)))
    hl_ref[...] = h
    for j in range(CONV_W - 1):
        cout_ref[j] = xp[nt + j]

    vn = []
    for t in range(nt):
        v = _layer_norm(jax.nn.gelu(rows(t, 3)), vng_ref[...], vnb_ref[...])
        vn_ref[t] = v
        vn.append(v)
    for t in range(nt):
        mix = b4_ref[t:t + 1]
        for s in range(t + 1):
            mix = mix + w4_ref[t, s:s + 1] * vn[s]
        put_rows(yb_ref, t, jax.nn.gelu(rows(t, 2)) * mix)


def _sample_mix(za, cbuf, h0, prm, l, nb, nt):
    m = nb * nt
    full = lambda shape: pl.BlockSpec(shape, lambda i: (0,) * len(shape))
    lsel = lambda shape: pl.BlockSpec((None,) + shape, lambda i: (l,) + (0,) * len(shape))
    return pl.pallas_call(
        _sample_mix_kernel,
        grid=(1,),
        in_specs=[
            full((m * 4 * MIX_W // LANES, LANES)),
            lsel((CONV_W - 1, nb, W_A)),
            lsel((nb, W_A)),
            lsel((CONV_W, W_A)), lsel((1, W_A)),
            lsel((H_A, BW_A, BW_A)), lsel((1, W_A)),
            lsel((H_A, BW_A, BW_A)), lsel((1, W_A)),
            lsel((1, W_A)),
            lsel((1, W_B)), lsel((1, W_B)),
            lsel((nt, nt, W_B)), lsel((nt, W_B)),
        ],
        out_specs=[
            full((m * W_A // LANES, LANES)), full((m * W_B // LANES, LANES)), full((nb, W_A)),
            full((CONV_W - 1, nb, W_A)), full((nt, nb, W_B)),
        ],
        out_shape=[
            jax.ShapeDtypeStruct((m * W_A // LANES, LANES), F32),
            jax.ShapeDtypeStruct((m * W_B // LANES, LANES), F32),
            jax.ShapeDtypeStruct((nb, W_A), F32),
            jax.ShapeDtypeStruct((CONV_W - 1, nb, W_A), F32),
            jax.ShapeDtypeStruct((nt, nb, W_B), F32),
        ],
        compiler_params=_cparams(("arbitrary",)),
        name="sample_mix",
    )(za, cbuf, h0, prm["conv_w"], prm["conv_b"], prm["w_a"], prm["b_a"], prm["w_x"], prm["b_x"],
      prm["lru_lam"], prm["vn_g"], prm["vn_b"], prm["w4"], prm["b4"])


def _expand_queries(q):
    hc = lax.broadcasted_iota(jnp.int32, (2 * H_C, W_C), 0)
    blk = lax.broadcasted_iota(jnp.int32, (2 * H_C, W_C), 1) // HD_C
    keep = hc == blk
    return jnp.concatenate(
        [jnp.where(keep, jnp.broadcast_to(q[t:t + 1], (2 * H_C, W_C)), 0.0) for t in range(q.shape[0])],
        axis=0)


def _page_keys(ref, page):
    return jnp.concatenate(
        [ref[pl.ds(_k_piece(hc), page, stride=HEAD_ROWS), :] for hc in range(2 * H_C)], axis=-1)


def _page_values(ref, page):
    return jnp.concatenate(
        [ref[pl.ds(_v_piece(h, half), page, stride=HEAD_ROWS), :] for h in range(H_C) for half in range(2)],
        axis=-1)


def _paged_attn_kernel(pt_ref, q_ref, kn_ref, vn_ref, lq1_ref, lk1_ref, lq2_ref, lk2_ref, g_ref, *rest,
                       pages_per_step, lambda_init):
    del pt_ref
    k_refs = rest[:pages_per_step]
    v_refs = rest[pages_per_step:2 * pages_per_step]
    o_ref, qe_ref, m_ref, l_ref, acc_ref = rest[2 * pages_per_step:]
    j = pl.program_id(1)
    nt = q_ref.shape[0]
    page = k_refs[0].shape[0] // HEAD_ROWS

    @pl.when(j == 0)
    def _():
        qe_ref[...] = _expand_queries(q_ref[...]).astype(BF16)
        m_ref[...] = jnp.full(m_ref.shape, NEG_INF, F32)
        l_ref[...] = jnp.zeros(l_ref.shape, F32)
        acc_ref[...] = jnp.zeros(acc_ref.shape, F32)

    qe = qe_ref[...]
    s = jnp.concatenate(
        [lax.dot_general(qe, _page_keys(k_refs[p], page).astype(BF16), (((1,), (1,)), ((), ())),
                         preferred_element_type=F32) for p in range(pages_per_step)], axis=-1)
    m_prev = m_ref[...]
    m_new = jnp.maximum(m_prev, jnp.max(s, axis=-1, keepdims=True))
    alpha = jnp.exp2(m_prev - m_new)
    p = jnp.exp2(s - m_new)
    l_ref[...] = alpha * l_ref[...] + jnp.sum(p, axis=-1, keepdims=True)
    pb = p.astype(BF16)
    pv = jnp.dot(pb[:, 0:page], _page_values(v_refs[0], page).astype(BF16), preferred_element_type=F32)
    for i in range(1, pages_per_step):
        pv = pv + jnp.dot(pb[:, i * page:(i + 1) * page], _page_values(v_refs[i], page).astype(BF16),
                          preferred_element_type=F32)
    acc_ref[...] = alpha * acc_ref[...] + pv
    m_ref[...] = m_new

    @pl.when(j == pl.num_programs(1) - 1)
    def _():
        qe32 = _expand_queries(q_ref[...])
        trow = lax.broadcasted_iota(jnp.int32, (nt * 2 * H_C, 1), 0) // (2 * H_C)
        kn = kn_ref[...]
        vn = vn_ref[...]
        sc = [jnp.where(trow >= t, jnp.sum(qe32 * kn[t:t + 1], axis=-1, keepdims=True), NEG_INF)
              for t in range(nt)]
        m_prev = m_ref[...]
        m_new = m_prev
        for t in range(nt):
            m_new = jnp.maximum(m_new, sc[t])
        alpha = jnp.exp2(m_prev - m_new)
        lsum = alpha * l_ref[...]
        acc = alpha * acc_ref[...]
        for t in range(nt):
            pt = jnp.exp2(sc[t] - m_new)
            lsum = lsum + pt
            acc = acc + pt * vn[t:t + 1]
        accn = acc / lsum
        lam = _diff_lambda(lq1_ref[...], lk1_ref[...], lq2_ref[...], lk2_ref[...], lambda_init)
        hc = lax.broadcasted_iota(jnp.int32, (nt * 2 * H_C, 1), 0) % (2 * H_C)
        outs = []
        for h in range(H_C):
            w = jnp.where(hc == 2 * h, 1.0, jnp.where(hc == 2 * h + 1, -lam, 0.0))
            o = jnp.sum((accn[:, h * DV_C:(h + 1) * DV_C] * w).reshape(nt, 2 * H_C, DV_C), axis=1)
            ms = jnp.mean(o * o, axis=-1, keepdims=True)
            outs.append(o * lax.rsqrt(ms + LN_EPS) * g_ref[...] * (1.0 - lambda_init))
        o_ref[...] = jnp.concatenate(outs, axis=-1)


def _paged_attn(page_table, qkv3, cache_k, cache_v, prm, l, pages_per_step, lambda_init):
    nb, nt, _ = qkv3.shape
    n_pages = page_table.shape[0] // nb
    page_rows = cache_k.shape[2]
    lamspec = pl.BlockSpec((None, 1, HD_C), lambda b, j, pt: (l, 0, 0))

    def page_spec(p):
        return pl.BlockSpec((None, None, page_rows, LANES),
                            lambda b, j, pt: (l, pt[b * n_pages + j * pages_per_step + p], 0, 0))

    grid_spec = pltpu.PrefetchScalarGridSpec(
        num_scalar_prefetch=1,
        grid=(nb, n_pages // pages_per_step),
        in_specs=[
            pl.BlockSpec((None, nt, W_C), lambda b, j, pt: (b, 0, 0)),
            pl.BlockSpec((None, nt, W_C), lambda b, j, pt: (b, 0, 1)),
            pl.BlockSpec((None, nt, W_C), lambda b, j, pt: (b, 0, 2)),
            lamspec, lamspec, lamspec, lamspec,
            pl.BlockSpec((None, 1, DV_C), lambda b, j, pt: (l, 0, 0)),
        ] + [page_spec(p) for p in range(pages_per_step)] * 2,
        out_specs=pl.BlockSpec((None, nt, W_C), lambda b, j, pt: (b, 0, 0)),
        scratch_shapes=[
            pltpu.VMEM((nt * 2 * H_C, W_C), BF16),
            pltpu.VMEM((nt * 2 * H_C, 1), F32),
            pltpu.VMEM((nt * 2 * H_C, 1), F32),
            pltpu.VMEM((nt * 2 * H_C, W_C), F32),
        ],
    )
    return pl.pallas_call(
        functools.partial(_paged_attn_kernel, pages_per_step=pages_per_step, lambda_init=lambda_init),
        grid_spec=grid_spec,
        out_shape=jax.ShapeDtypeStruct((nb, nt, W_C), F32),
        compiler_params=_cparams(("arbitrary", "arbitrary")),
        name="paged_attn",
    )(page_table, qkv3, qkv3, qkv3, prm["lam_q1"], prm["lam_k1"], prm["lam_q2"], prm["lam_k2"],
      prm["subln_g"], *([cache_k] * pages_per_step), *([cache_v] * pages_per_step))


def _values_to_head_rows(v):
    lead = v.shape[:-3]
    tokens = v.shape[-3]
    v = v.reshape(lead + (tokens, H_C, 2, LANES))
    v = jnp.swapaxes(v, -3, -2)
    return v.reshape(lead + (tokens * HEAD_ROWS, LANES))


def _head_rows_to_values(r, lead):
    depth = r.shape[0]
    v = r.reshape((depth,) + lead + (2, H_C, LANES))
    v = jnp.swapaxes(v, -3, -2)
    return v.reshape((depth,) + lead + (H_C, DV_C))


def _largest_tile(n, cap, mult):
    t = min(n, cap)
    while n % t or t % mult:
        t -= 1
    return t


def kernel(x_prompt, x_sample, cache_k, cache_v, state_lru_h, state_conv, page_table, w_in, w_out, conv_w,
           conv_b, w_a, b_a, w_x, b_x, lru_lam, vn_g, vn_b, w_s, b_s, lam_q1, lam_k1, lam_q2, lam_k2,
           subln_g, ln1_g, ln1_b, ln2_g, ln2_b, w_gate, w_up, w_down):
    depth = w_in.shape[0]
    nbp, seq, _ = x_prompt.shape
    nbs, nts, _ = x_sample.shape
    n_pool, page = cache_k.shape[1], cache_k.shape[2]
    n_pages = page_table.shape[1]
    alpha = (2.0 * depth) ** 0.25
    mp, ms = nbp * seq, nbs * nts

    row3 = lambda a: a.reshape(depth, 1, a.shape[-1])
    prm = {
        "w_a": w_a.astype(BF16), "w_x": w_x.astype(BF16),
        "conv_w": conv_w, "conv_b": row3(conv_b), "b_a": row3(b_a), "b_x": row3(b_x),
        "lru_lam": row3(lru_lam), "vn_g": row3(vn_g), "vn_b": row3(vn_b), "w_s": w_s,
        "bs_rows": jnp.repeat(jnp.transpose(b_s, (0, 2, 1)), GW_B, axis=-1),
        "w4": jnp.repeat(jnp.transpose(w_s[:, :, :nts, :nts], (0, 2, 3, 1)), GW_B, axis=-1),
        "b4": jnp.repeat(jnp.transpose(b_s[:, :, :nts], (0, 2, 1)), GW_B, axis=-1),
        "lam_q1": row3(lam_q1), "lam_k1": row3(lam_k1), "lam_q2": row3(lam_q2), "lam_k2": row3(lam_k2),
        "subln_g": row3(subln_g), "ln1_g": row3(ln1_g), "ln1_b": row3(ln1_b),
        "ln2_g": row3(ln2_g), "ln2_b": row3(ln2_b),
    }
    ck = cache_k.reshape(depth, n_pool, page * HEAD_ROWS, LANES)
    cv = _values_to_head_rows(cache_v)
    pt_flat = page_table.reshape(-1)
    sconv = jnp.transpose(state_conv, (0, 2, 1, 3))
    zero_cbuf = jnp.zeros((nbp, SUBLANES, W_A), F32)
    zero_h = jnp.zeros((nbp, 1, W_A), F32)

    tm_in = _largest_tile(mp, 1024, CHUNK)
    tm_p = _largest_tile(mp, 512, CHUNK)
    tt_a = _largest_tile(seq, 256, SUBLANES)
    tk = _largest_tile(seq, 512, CHUNK)
    tq = _largest_tile(tk, 512, CHUNK)
    tf = _largest_tile(D_FF, 512, 128)
    pps = _largest_tile(n_pages, 16, 1)

    xp = x_prompt.reshape(mp, D_MODEL)
    xs = x_sample.reshape(ms, D_MODEL)
    xp_b, xs_b = xp.astype(BF16), xs.astype(BF16)
    kbuf_p = jnp.zeros((depth, mp * HEAD_ROWS, LANES), F32)
    vbuf_p = jnp.zeros((depth, mp * HEAD_ROWS, LANES), F32)
    kbuf_s = jnp.zeros((depth, ms * HEAD_ROWS, LANES), F32)
    vbuf_s = jnp.zeros((depth, ms * HEAD_ROWS, LANES), F32)
    outs = [[] for _ in range(5)]
    for l in range(depth):
        lambda_init = 0.8 - 0.6 * math.exp(-0.3 * l)
        za_s, qkv_s, kbuf_s, vbuf_s, w_in_b = _inproj(xs_b, w_in, l, kbuf_s, vbuf_s, l, ms, F32, True)
        ya_s, yb_s, h_s, c_s, vn_s = _sample_mix(za_s.reshape(-1, LANES), sconv, state_lru_h, prm, l, nbs,
                                                  nts)
        ya_s, yb_s = ya_s.reshape(ms, W_A), yb_s.reshape(ms, W_B)
        o_s = _paged_attn(pt_flat, qkv_s.reshape(nbs, nts, 3 * W_C), ck, cv, prm, l, pps, lambda_init)
        xs, xs_b, w_out_b = _outproj(ya_s, yb_s, o_s.reshape(ms, W_C), xs, w_out, l, prm, l, ms, alpha, True)
        xs, xs_b, wg_b, wu_b, wd_b = _swiglu(xs, xs_b, w_gate, w_up, w_down, l, prm, l, ms, tf, alpha, True)
        za, qkv, kbuf_p, vbuf_p = _inproj(xp_b, w_in_b, 0, kbuf_p, vbuf_p, l, tm_in, BF16, False)
        ya, h_p, c_p = _mix_a(za, zero_cbuf, zero_h, prm, l, nbp, tt_a)
        yb = _mix_b(za, prm, l, tm_p)
        o = _attn(qkv, prm, l, nbp, seq, tq, tk, lambda_init)
        xp, xp_b = _outproj(ya, yb, o, xp, w_out_b, 0, prm, l, tm_p, alpha, False)
        xp, xp_b = _swiglu(xp, xp_b, wg_b, wu_b, wd_b, 0, prm, l, tm_p, tf, alpha, False)

        for lst, val in zip(outs, (h_p, h_s, c_p, c_s, vn_s)):
            lst.append(val)

    hp, hs, cp, cs, chv = [jnp.stack(v) for v in outs]
    return (
        xp.reshape(nbp, seq, D_MODEL),
        xs.reshape(nbs, nts, D_MODEL),
        kbuf_p.reshape(depth, nbp, seq, H_C, 2, HD_C),
        _head_rows_to_values(vbuf_p, (nbp, seq)),
        kbuf_s.reshape(depth, nbs, nts, H_C, 2, HD_C),
        _head_rows_to_values(vbuf_s, (nbs, nts)),
        hp.reshape(depth, nbp, W_A),
        hs,
        cp[:, :, SUBLANES - (CONV_W - 1):, :],
        jnp.transpose(cs, (0, 2, 1, 3)),
        jnp.transpose(chv, (0, 2, 1, 3)),
    )
```

```python
import functools
import math

import jax
import jax.numpy as jnp
from jax import lax
from jax.experimental import pallas as pl
from jax.experimental.pallas import tpu as pltpu

F32 = jnp.float32
BF16 = jnp.bfloat16

D_MODEL = 2048
W_A = 512
H_A = 4
BW_A = 128
CONV_W = 4
RG_C = 8.0
W_B = 512
G_B = 4
GW_B = 128
CHUNK = 128
W_C = 1024
H_C = 4
HD_C = 128
DV_C = 256
W_IN = 2 * W_A + 2 * W_B + 3 * W_C
D_FF = 5632
LN_EPS = 1e-5
NEG_INF = -1e30
QK_SCALE = HD_C ** -0.5 * math.log2(math.e)

SUBLANES = 8
LANES = 128
COL_TILE = 1024
MIX_W = 512
HEAD_ROWS = W_C // LANES
VMEM_LIMIT = 56 * 1024 * 1024


def _cparams(sem):
    return pltpu.CompilerParams(dimension_semantics=sem, vmem_limit_bytes=VMEM_LIMIT)


def _layer_norm(r, g, b):
    mu = jnp.mean(r, axis=-1, keepdims=True)
    d = r - mu
    var = jnp.mean(d * d, axis=-1, keepdims=True)
    return d * lax.rsqrt(var + LN_EPS) * g + b


def _softplus(x):
    return jnp.maximum(x, 0.0) + jnp.log1p(jnp.exp(-jnp.abs(x)))


def _neg_expm1_nonpos(x):
    t = jnp.tanh(0.5 * x)
    return -2.0 * t / (1.0 - t)


def _diff_lambda(lq1, lk1, lq2, lk2, lambda_init):
    s1 = jnp.sum(lq1 * lk1, axis=-1, keepdims=True)
    s2 = jnp.sum(lq2 * lk2, axis=-1, keepdims=True)
    return jnp.exp(s1) - jnp.exp(s2) + lambda_init


def _lane_tile(x, width):
    return jnp.concatenate([x] * (width // LANES), axis=-1)


def _k_piece(hc):
    return hc


def _v_piece(h, half):
    return half * H_C + h


def _inproj_kernel(x_ref, w_ref, kin_ref, vin_ref, za_ref, qkv_ref, kf_ref, vf_ref, *wb_refs):
    del kin_ref, vin_ref
    j = pl.program_id(1)
    tm = x_ref.shape[0]
    for wb_ref in wb_refs:
        wb_ref[...] = w_ref[...].astype(BF16)

    def project():
        return jnp.dot(x_ref[...], w_ref[...].astype(BF16), preferred_element_type=F32)

    @pl.when(j < 2)
    def _():
        za_ref[...] = project()

    @pl.when(j == 2)
    def _():
        qkv_ref[...] = (project() * QK_SCALE).astype(qkv_ref.dtype)

    @pl.when(j == 3)
    def _():
        z = project()
        qkv_ref[...] = z.astype(qkv_ref.dtype)
        for hc in range(2 * H_C):
            kf_ref[pl.ds(_k_piece(hc), tm, stride=HEAD_ROWS), :] = z[:, hc * HD_C:(hc + 1) * HD_C]

    @pl.when(j == 4)
    def _():
        z = project()
        qkv_ref[...] = z.astype(qkv_ref.dtype)
        for h in range(H_C):
            for half in range(2):
                c0 = h * DV_C + half * LANES
                vf_ref[pl.ds(_v_piece(h, half), tm, stride=HEAD_ROWS), :] = z[:, c0:c0 + LANES]


def _inproj(xb, w, wl, kbuf, vbuf, l, tm, qkv_dtype, emit_weights):
    m = xb.shape[0]
    nj = W_IN // COL_TILE
    kv_spec = pl.BlockSpec((None, tm * HEAD_ROWS, LANES), lambda i, j: (l, i, 0))
    out_specs = [
        pl.BlockSpec((tm, COL_TILE), lambda i, j: (i, jnp.minimum(j, 1))),
        pl.BlockSpec((tm, COL_TILE), lambda i, j: (i, jnp.clip(j - 2, 0, 2))),
        kv_spec, kv_spec,
    ]
    out_shape = [
        jax.ShapeDtypeStruct((m, 4 * MIX_W), F32),
        jax.ShapeDtypeStruct((m, 3 * W_C), qkv_dtype),
        jax.ShapeDtypeStruct(kbuf.shape, F32),
        jax.ShapeDtypeStruct(vbuf.shape, F32),
    ]
    if emit_weights:
        out_specs.append(pl.BlockSpec((None, D_MODEL, COL_TILE), lambda i, j: (0, 0, j)))
        out_shape.append(jax.ShapeDtypeStruct((1, D_MODEL, W_IN), BF16))
    return pl.pallas_call(
        _inproj_kernel,
        grid=(m // tm, nj),
        in_specs=[
            pl.BlockSpec((tm, D_MODEL), lambda i, j: (i, 0)),
            pl.BlockSpec((None, D_MODEL, COL_TILE), lambda i, j: (wl, 0, j)),
            pl.BlockSpec(memory_space=pl.ANY),
            pl.BlockSpec(memory_space=pl.ANY),
        ],
        out_specs=out_specs,
        out_shape=out_shape,
        input_output_aliases={2: 2, 3: 3},
        compiler_params=_cparams(("arbitrary", "arbitrary")),
        name="inproj",
    )(xb, w, kbuf, vbuf)


def _shift_rows(x, s, fill):
    n, w = x.shape
    if s % SUBLANES == 0:
        return jnp.concatenate([jnp.full((s, w), fill, x.dtype), x[:n - s]], axis=0)
    row = lax.broadcasted_iota(jnp.int32, x.shape, 0)
    return jnp.where(row < s, fill, pltpu.roll(x, s, 0))


def _block_gate(xcb, w_ref, b_ref):
    parts = [jnp.dot(xcb[:, h * BW_A:(h + 1) * BW_A], w_ref[h], preferred_element_type=F32)
             for h in range(H_A)]
    return jax.nn.sigmoid(jnp.concatenate(parts, axis=-1) + b_ref[...])


def _rglru_coeffs(xc, wa_ref, ba_ref, wx_ref, bx_ref, lam_ref):
    xcb = xc.astype(BF16)
    r = _block_gate(xcb, wa_ref, ba_ref)
    gi = _block_gate(xcb, wx_ref, bx_ref)
    log_a = -RG_C * r * _softplus(-lam_ref[...])
    a = jnp.exp(log_a)
    b = jnp.sqrt(_neg_expm1_nonpos(2.0 * log_a)) * (gi * xc)
    return a, b


def _mix_a_kernel(xa_ref, ga_ref, cbuf_ref, h0_ref, cw_ref, cb_ref, wa_ref, ba_ref, wx_ref, bx_ref,
                  lam_ref, ya_ref, hl_ref, cout_ref, tail_ref, hc_ref):
    t = pl.program_id(1)
    tt = xa_ref.shape[0]

    @pl.when(t == 0)
    def _():
        tail_ref[...] = cbuf_ref[...]
        hc_ref[...] = h0_ref[...]

    xa = xa_ref[...]
    tail = tail_ref[...]
    row8 = lax.broadcasted_iota(jnp.int32, tail.shape, 0)

    def delayed(k):
        r = pltpu.roll(xa, k, 0)
        first = jnp.where(row8 < k, pltpu.roll(tail, k, 0), r[0:SUBLANES])
        return jnp.concatenate([first, r[SUBLANES:]], axis=0)

    cw = cw_ref[...]
    xc = cb_ref[...] + cw[3:4] * xa
    for k in range(1, CONV_W):
        xc = xc + cw[CONV_W - 1 - k:CONV_W - k] * delayed(k)
    tail_ref[...] = xa[tt - SUBLANES:tt]

    a, b = _rglru_coeffs(xc, wa_ref, ba_ref, wx_ref, bx_ref, lam_ref)
    s = 1
    while s < tt:
        b = b + a * _shift_rows(b, s, 0.0)
        a = a * _shift_rows(a, s, 1.0)
        s *= 2
    h = a * hc_ref[...] + b
    hc_ref[...] = h[tt - 1:tt]
    ya_ref[...] = (h * jax.nn.gelu(ga_ref[...])).astype(ya_ref.dtype)

    @pl.when(t == pl.num_programs(1) - 1)
    def _():
        hl_ref[...] = h[tt - 1:tt]
        cout_ref[...] = xa[tt - SUBLANES:tt]


def _mix_a(za, cbuf8, h0, prm, l, nb, tt):
    m = za.shape[0]
    nt = m // nb // tt
    vec = pl.BlockSpec((None, 1, W_A), lambda b, t: (l, 0, 0))
    blk = pl.BlockSpec((None, H_A, BW_A, BW_A), lambda b, t: (l, 0, 0, 0))
    return pl.pallas_call(
        _mix_a_kernel,
        grid=(nb, nt),
        in_specs=[
            pl.BlockSpec((tt, W_A), lambda b, t: (b * nt + t, 0)),
            pl.BlockSpec((tt, W_A), lambda b, t: (b * nt + t, 1)),
            pl.BlockSpec((None, SUBLANES, W_A), lambda b, t: (b, 0, 0)),
            pl.BlockSpec((None, 1, W_A), lambda b, t: (b, 0, 0)),
            pl.BlockSpec((None, CONV_W, W_A), lambda b, t: (l, 0, 0)),
            vec, blk, vec, blk, vec, vec,
        ],
        out_specs=[
            pl.BlockSpec((tt, W_A), lambda b, t: (b * nt + t, 0)),
            pl.BlockSpec((None, 1, W_A), lambda b, t: (b, 0, 0)),
            pl.BlockSpec((None, SUBLANES, W_A), lambda b, t: (b, 0, 0)),
        ],
        out_shape=[
            jax.ShapeDtypeStruct((m, W_A), BF16),
            jax.ShapeDtypeStruct((nb, 1, W_A), F32),
            jax.ShapeDtypeStruct((nb, SUBLANES, W_A), F32),
        ],
        scratch_shapes=[pltpu.VMEM((SUBLANES, W_A), F32), pltpu.VMEM((1, W_A), F32)],
        compiler_params=_cparams(("arbitrary", "arbitrary")),
        name="mix_a",
    )(za, za, cbuf8, h0, prm["conv_w"], prm["conv_b"], prm["w_a"], prm["b_a"], prm["w_x"], prm["b_x"],
      prm["lru_lam"])


def _mix_b_kernel(ub_ref, vb_ref, ws_ref, bs_ref, g_ref, b_ref, yb_ref):
    tt = ub_ref.shape[0]
    vn = _layer_norm(jax.nn.gelu(vb_ref[...]), g_ref[...], b_ref[...]).astype(BF16)
    row = lax.broadcasted_iota(jnp.int32, (CHUNK, CHUNK), 0)
    col = lax.broadcasted_iota(jnp.int32, (CHUNK, CHUNK), 1)
    wt = [jnp.where(col <= row, ws_ref[g], 0.0).astype(BF16) for g in range(G_B)]
    bias = bs_ref[...]
    chunks = []
    for c in range(tt // CHUNK):
        vc = vn[c * CHUNK:(c + 1) * CHUNK]
        parts = [jnp.dot(wt[g], vc[:, g * GW_B:(g + 1) * GW_B], preferred_element_type=F32)
                 for g in range(G_B)]
        chunks.append(jnp.concatenate(parts, axis=-1) + bias)
    mix = jnp.concatenate(chunks, axis=0)
    yb_ref[...] = (jax.nn.gelu(ub_ref[...]) * mix).astype(yb_ref.dtype)


def _mix_b(za, prm, l, tt):
    m = za.shape[0]
    return pl.pallas_call(
        _mix_b_kernel,
        grid=(m // tt,),
        in_specs=[
            pl.BlockSpec((tt, W_B), lambda i: (i, 2)),
            pl.BlockSpec((tt, W_B), lambda i: (i, 3)),
            pl.BlockSpec((None, G_B, CHUNK, CHUNK), lambda i: (l, 0, 0, 0)),
            pl.BlockSpec((None, CHUNK, W_B), lambda i: (l, 0, 0)),
            pl.BlockSpec((None, 1, W_B), lambda i: (l, 0, 0)),
            pl.BlockSpec((None, 1, W_B), lambda i: (l, 0, 0)),
        ],
        out_specs=pl.BlockSpec((tt, W_B), lambda i: (i, 0)),
        out_shape=jax.ShapeDtypeStruct((m, W_B), BF16),
        compiler_params=_cparams(("arbitrary",)),
        name="mix_b",
    )(za, za, prm["w_s"], prm["bs_rows"], prm["vn_g"], prm["vn_b"])


def _attn_kernel(q_ref, k_ref, v_ref, lq1_ref, lk1_ref, lq2_ref, lk2_ref, g_ref, o_ref,
                 m_ref, l_ref, acc_ref, p_ref, a_ref, *, lambda_init, tk):
    qi = pl.program_id(2)
    tq = q_ref.shape[0]
    m_ref[...] = jnp.full(m_ref.shape, NEG_INF, F32)
    l_ref[...] = jnp.zeros(l_ref.shape, F32)
    acc_ref[...] = jnp.zeros(acc_ref.shape, F32)
    q = q_ref[...]

    def scores(kstart, slot, masked):
        k = k_ref[pl.ds(kstart, tk), :]
        if masked:
            row = lax.broadcasted_iota(jnp.int32, (tq, tk), 0) + qi * tq
            col = lax.broadcasted_iota(jnp.int32, (tq, tk), 1) + kstart
            visible = col <= row
        for c in range(2):
            s = lax.dot_general(q[:, c * HD_C:(c + 1) * HD_C], k[:, c * HD_C:(c + 1) * HD_C],
                                (((1,), (1,)), ((), ())), preferred_element_type=F32)
            if masked:
                s = jnp.where(visible, s, NEG_INF)
            m_prev = m_ref[c]
            m_new = jnp.maximum(m_prev, jnp.max(s, axis=-1, keepdims=True))
            alpha = jnp.exp2(m_prev - m_new)
            p = jnp.exp2(s - _lane_tile(m_new, tk))
            l_ref[c] = alpha * l_ref[c] + jnp.sum(p, axis=-1, keepdims=True)
            m_ref[c] = m_new
            p_ref[slot, c] = p.astype(BF16)
            a_ref[slot, c] = alpha

    def accumulate(vstart, slot):
        v = v_ref[pl.ds(vstart, tk), :]
        for c in range(2):
            acc_ref[c] = _lane_tile(a_ref[slot, c], DV_C) * acc_ref[c] + jnp.dot(
                p_ref[slot, c], v, preferred_element_type=F32)

    n_full = (qi * tq) // tk
    diag = pl.multiple_of(n_full * tk, tk)
    scores(diag, 0, True)

    def body(i, pending):
        vstart, slot = pending
        kstart = pl.multiple_of(i * tk, tk)
        accumulate(pl.multiple_of(vstart, tk), slot)
        scores(kstart, 1 - slot, False)
        return kstart, 1 - slot

    def unrolled(count):
        def multi_body(i, pending):
            for u in range(count):
                pending = body(count * i + u, pending)
            return pending
        return multi_body

    pending = (diag, jnp.int32(0))
    done = 0
    for count in (2, 1):
        stop = done + ((n_full - done) // count) * count
        pending = lax.fori_loop(done // count, stop // count, unrolled(count), pending)
        done = stop
    vstart, slot = pending
    accumulate(pl.multiple_of(vstart, tk), slot)

    lam = _diff_lambda(lq1_ref[...], lk1_ref[...], lq2_ref[...], lk2_ref[...], lambda_init)
    o = acc_ref[0] / _lane_tile(l_ref[0], DV_C) - lam * (acc_ref[1] / _lane_tile(l_ref[1], DV_C))
    ms = jnp.mean(o * o, axis=-1, keepdims=True)
    o_ref[...] = (o * lax.rsqrt(ms + LN_EPS) * g_ref[...] * (1.0 - lambda_init)).astype(o_ref.dtype)


def _outproj_kernel(ya_ref, yb_ref, o_ref, x_ref, w_ref, g_ref, b_ref, out_ref, outb_ref, *wb_refs, alpha):
    w = w_ref[...].astype(BF16)
    for wb_ref in wb_refs:
        wb_ref[...] = w
    half = x_ref.shape[0] // 2
    for rows in (pl.ds(0, half), pl.ds(half, half)):
        cat = jnp.concatenate([ya_ref[rows, :].astype(BF16), yb_ref[rows, :].astype(BF16),
                               o_ref[rows, :].astype(BF16)], axis=-1)
        y = jnp.dot(cat, w, preferred_element_type=F32)
        r = _layer_norm(alpha * x_ref[rows, :] + y, g_ref[...], b_ref[...])
        out_ref[rows, :] = r
        outb_ref[rows, :] = r.astype(BF16)


def _outproj(ya, yb, o, x, w, wl, prm, l, tm, alpha, emit_weights):
    m = x.shape[0]
    vec = pl.BlockSpec((None, 1, D_MODEL), lambda i: (l, 0, 0))
    row = pl.BlockSpec((tm, D_MODEL), lambda i: (i, 0))
    wspec = pl.BlockSpec((None, D_MODEL, D_MODEL), lambda i: (wl, 0, 0), pipeline_mode=pl.Buffered(1))
    out_specs = [row, row]
    out_shape = [jax.ShapeDtypeStruct((m, D_MODEL), F32), jax.ShapeDtypeStruct((m, D_MODEL), BF16)]
    if emit_weights:
        out_specs.append(pl.BlockSpec((None, D_MODEL, D_MODEL), lambda i: (0, 0, 0)))
        out_shape.append(jax.ShapeDtypeStruct((1, D_MODEL, D_MODEL), BF16))
    return pl.pallas_call(
        functools.partial(_outproj_kernel, alpha=alpha),
        grid=(m // tm,),
        in_specs=[
            pl.BlockSpec((tm, W_A), lambda i: (i, 0)),
            pl.BlockSpec((tm, W_B), lambda i: (i, 0)),
            pl.BlockSpec((tm, W_C), lambda i: (i, 0)),
            row,
            wspec,
            vec, vec,
        ],
        out_specs=out_specs,
        out_shape=out_shape,
        compiler_params=_cparams(("arbitrary",)),
        name="outproj",
    )(ya, yb, o, x, w, prm["ln1_g"], prm["ln1_b"])


def _swiglu_kernel(x_ref, xb_ref, wg_ref, wu_ref, wd_ref, g_ref, b_ref, out_ref, outb_ref, *rest, alpha):
    *wb_refs, acc_ref = rest
    f = pl.program_id(1)

    @pl.when(f == 0)
    def _():
        acc_ref[...] = jnp.zeros(acc_ref.shape, F32)

    for wb_ref, w_ref in zip(wb_refs, (wg_ref, wu_ref, wd_ref)):
        wb_ref[...] = w_ref[...].astype(BF16)

    def ffn(rows):
        xb = xb_ref[rows, :]
        hg = jnp.dot(xb, wg_ref[...].astype(BF16), preferred_element_type=F32)
        hu = jnp.dot(xb, wu_ref[...].astype(BF16), preferred_element_type=F32)
        h = (hg * jax.nn.sigmoid(hg)) * hu
        return jnp.dot(h.astype(BF16), wd_ref[...].astype(BF16), preferred_element_type=F32)

    last = pl.num_programs(1) - 1
    tm = x_ref.shape[0]

    @pl.when(f < last)
    def _():
        acc_ref[...] += ffn(pl.ds(0, tm))

    @pl.when(f == last)
    def _():
        half = tm // 2
        for rows in (pl.ds(0, half), pl.ds(half, half)):
            y = acc_ref[rows, :] + ffn(rows)
            r = _layer_norm(alpha * x_ref[rows, :] + y, g_ref[...], b_ref[...])
            out_ref[rows, :] = r
            outb_ref[rows, :] = r.astype(BF16)


def _swiglu(x, xb, wg, wu, wd, wl, prm, l, tm, tf, alpha, emit_weights):
    m = x.shape[0]
    vec = pl.BlockSpec((None, 1, D_MODEL), lambda i, f: (l, 0, 0))
    row = pl.BlockSpec((tm, D_MODEL), lambda i, f: (i, 0))
    col_tile = lambda layer: pl.BlockSpec((None, D_MODEL, tf), lambda i, f: (layer, 0, f))
    row_tile = lambda layer: pl.BlockSpec((None, tf, D_MODEL), lambda i, f: (layer, f, 0))
    out_specs = [row, row]
    out_shape = [jax.ShapeDtypeStruct((m, D_MODEL), F32), jax.ShapeDtypeStruct((m, D_MODEL), BF16)]
    if emit_weights:
        out_specs += [col_tile(0), col_tile(0), row_tile(0)]
        out_shape += [jax.ShapeDtypeStruct((1, D_MODEL, D_FF), BF16)] * 2
        out_shape += [jax.ShapeDtypeStruct((1, D_FF, D_MODEL), BF16)]
    return pl.pallas_call(
        functools.partial(_swiglu_kernel, alpha=alpha),
        grid=(m // tm, D_FF // tf),
        in_specs=[row, row, col_tile(wl), col_tile(wl), row_tile(wl), vec, vec],
        out_specs=out_specs,
        out_shape=out_shape,
        scratch_shapes=[pltpu.VMEM((tm, D_MODEL), F32)],
        compiler_params=_cparams(("arbitrary", "arbitrary")),
        name="swiglu",
    )(x, xb, wg, wu, wd, prm["ln2_g"], prm["ln2_b"])


def _sample_mix_kernel(za_ref, cbuf_ref, h0_ref, cw_ref, cb_ref, wa_ref, ba_ref, wx_ref, bx_ref, lam_ref,
                       vng_ref, vnb_ref, w4_ref, b4_ref,
                       ya_ref, yb_ref, hl_ref, cout_ref, vn_ref):
    nt = vn_ref.shape[0]
    nb = h0_ref.shape[0]
    zc = 4 * MIX_W // LANES
    gc = MIX_W // LANES

    def rows(t, grp):
        return jnp.concatenate(
            [za_ref[pl.ds(t * zc + grp * gc + c, nb, stride=nt * zc), :] for c in range(gc)], axis=-1)

    def put_rows(ref, t, val):
        for c in range(gc):
            ref[pl.ds(t * gc + c, nb, stride=nt * gc), :] = val[:, c * LANES:(c + 1) * LANES]

    xa = [rows(t, 0) for t in range(nt)]
    xp = [cbuf_ref[j] for j in range(CONV_W - 1)] + xa
    cw = cw_ref[...]
    xc = []
    for t in range(nt):
        acc = cb_ref[...] + cw[0:1] * xp[t]
        for j in range(1, CONV_W):
            acc = acc + cw[j:j + 1] * xp[t + j]
        xc.append(acc)
    a, b = _rglru_coeffs(jnp.concatenate(xc, axis=0), wa_ref, ba_ref, wx_ref, bx_ref, lam_ref)
    h = h0_ref[...]
    for t in range(nt):
        h = a[t * nb:(t + 1) * nb] * h + b[t * nb:(t + 1) * nb]
        put_rows(ya_ref, t, h * jax.nn.gelu(rows(t, 1)))
    hl_ref[...] = h
    for j in range(CONV_W - 1):
        cout_ref[j] = xp[nt + j]

    vn = []
    for t in range(nt):
        v = _layer_norm(jax.nn.gelu(rows(t, 3)), vng_ref[...], vnb_ref[...])
        vn_ref[t] = v
        vn.append(v)
    for t in range(nt):
        mix = b4_ref[t:t + 1]
        for s in range(t + 1):
            mix = mix + w4_ref[t, s:s + 1] * vn[s]
        put_rows(yb_ref, t, jax.nn.gelu(rows(t, 2)) * mix)


def _sample_mix(za, cbuf, h0, prm, l, nb, nt):
    m = nb * nt
    full = lambda shape: pl.BlockSpec(shape, lambda i: (0,) * len(shape))
    lsel = lambda shape: pl.BlockSpec((None,) + shape, lambda i: (l,) + (0,) * len(shape))
    return pl.pallas_call(
        _sample_mix_kernel,
        grid=(1,),
        in_specs=[
            full((m * 4 * MIX_W // LANES, LANES)),
            lsel((CONV_W - 1, nb, W_A)),
            lsel((nb, W_A)),
            lsel((CONV_W, W_A)), lsel((1, W_A)),
            lsel((H_A, BW_A, BW_A)), lsel((1, W_A)),
            lsel((H_A, BW_A, BW_A)), lsel((1, W_A)),
            lsel((1, W_A)),
            lsel((1, W_B)), lsel((1, W_B)),
            lsel((nt, nt, W_B)), lsel((nt, W_B)),
        ],
        out_specs=[
            full((m * W_A // LANES, LANES)), full((m * W_B // LANES, LANES)), full((nb, W_A)),
            full((CONV_W - 1, nb, W_A)), full((nt, nb, W_B)),
        ],
        out_shape=[
            jax.ShapeDtypeStruct((m * W_A // LANES, LANES), F32),
            jax.ShapeDtypeStruct((m * W_B // LANES, LANES), F32),
            jax.ShapeDtypeStruct((nb, W_A), F32),
            jax.ShapeDtypeStruct((CONV_W - 1, nb, W_A), F32),
            jax.ShapeDtypeStruct((nt, nb, W_B), F32),
        ],
        compiler_params=_cparams(("arbitrary",)),
        name="sample_mix",
    )(za, cbuf, h0, prm["conv_w"], prm["conv_b"], prm["w_a"], prm["b_a"], prm["w_x"], prm["b_x"],
      prm["lru_lam"], prm["vn_g"], prm["vn_b"], prm["w4"], prm["b4"])


def _expand_queries(q):
    hc = lax.broadcasted_iota(jnp.int32, (2 * H_C, W_C), 0)
    blk = lax.broadcasted_iota(jnp.int32, (2 * H_C, W_C), 1) // HD_C
    keep = hc == blk
    return jnp.concatenate(
        [jnp.where(keep, jnp.broadcast_to(q[t:t + 1], (2 * H_C, W_C)), 0.0) for t in range(q.shape[0])],
        axis=0)


def _page_keys(ref, page):
    return jnp.concatenate(
        [ref[pl.ds(_k_piece(hc), page, stride=HEAD_ROWS), :] for hc in range(2 * H_C)], axis=-1)


def _page_values(ref, page):
    return jnp.concatenate(
        [ref[pl.ds(_v_piece(h, half), page, stride=HEAD_ROWS), :] for h in range(H_C) for half in range(2)],
        axis=-1)


def _paged_attn_kernel(pt_ref, q_ref, kn_ref, vn_ref, lq1_ref, lk1_ref, lq2_ref, lk2_ref, g_ref, *rest,
                       pages_per_step, lambda_init, step_axis):
    del pt_ref
    k_refs = rest[:pages_per_step]
    v_refs = rest[pages_per_step:2 * pages_per_step]
    o_ref, qe_ref, m_ref, l_ref, acc_ref = rest[2 * pages_per_step:]
    j = pl.program_id(step_axis)
    nt = q_ref.shape[0]
    page = k_refs[0].shape[0] // HEAD_ROWS

    @pl.when(j == 0)
    def _():
        qe_ref[...] = _expand_queries(q_ref[...]).astype(BF16)
        m_ref[...] = jnp.full(m_ref.shape, NEG_INF, F32)
        l_ref[...] = jnp.zeros(l_ref.shape, F32)
        acc_ref[...] = jnp.zeros(acc_ref.shape, F32)

    qe = qe_ref[...]
    s = jnp.concatenate(
        [lax.dot_general(qe, _page_keys(k_refs[p], page).astype(BF16), (((1,), (1,)), ((), ())),
                         preferred_element_type=F32) for p in range(pages_per_step)], axis=-1)
    m_prev = m_ref[...]
    m_new = jnp.maximum(m_prev, jnp.max(s, axis=-1, keepdims=True))
    alpha = jnp.exp2(m_prev - m_new)
    p = jnp.exp2(s - m_new)
    l_ref[...] = alpha * l_ref[...] + jnp.sum(p, axis=-1, keepdims=True)
    pb = p.astype(BF16)
    pv = jnp.dot(pb[:, 0:page], _page_values(v_refs[0], page).astype(BF16), preferred_element_type=F32)
    for i in range(1, pages_per_step):
        pv = pv + jnp.dot(pb[:, i * page:(i + 1) * page], _page_values(v_refs[i], page).astype(BF16),
                          preferred_element_type=F32)
    acc_ref[...] = alpha * acc_ref[...] + pv
    m_ref[...] = m_new

    @pl.when(j == pl.num_programs(step_axis) - 1)
    def _():
        qe32 = _expand_queries(q_ref[...])
        trow = lax.broadcasted_iota(jnp.int32, (nt * 2 * H_C, 1), 0) // (2 * H_C)
        kn = kn_ref[...]
        vn = vn_ref[...]
        sc = [jnp.where(trow >= t, jnp.sum(qe32 * kn[t:t + 1], axis=-1, keepdims=True), NEG_INF)
              for t in range(nt)]
        m_prev = m_ref[...]
        m_new = m_prev
        for t in range(nt):
            m_new = jnp.maximum(m_new, sc[t])
        alpha = jnp.exp2(m_prev - m_new)
        lsum = alpha * l_ref[...]
        acc = alpha * acc_ref[...]
        for t in range(nt):
            pt = jnp.exp2(sc[t] - m_new)
            lsum = lsum + pt
            acc = acc + pt * vn[t:t + 1]
        accn = acc / lsum
        lam = _diff_lambda(lq1_ref[...], lk1_ref[...], lq2_ref[...], lk2_ref[...], lambda_init)
        hc = lax.broadcasted_iota(jnp.int32, (nt * 2 * H_C, 1), 0) % (2 * H_C)
        outs = []
        for h in range(H_C):
            w = jnp.where(hc == 2 * h, 1.0, jnp.where(hc == 2 * h + 1, -lam, 0.0))
            o = jnp.sum((accn[:, h * DV_C:(h + 1) * DV_C] * w).reshape(nt, 2 * H_C, DV_C), axis=1)
            ms = jnp.mean(o * o, axis=-1, keepdims=True)
            outs.append(o * lax.rsqrt(ms + LN_EPS) * g_ref[...] * (1.0 - lambda_init))
        o_ref[...] = jnp.concatenate(outs, axis=-1)


N_ATTN_SCRATCH = 5


def _attn_both_kernel(pt_ref, q_ref, k_ref, v_ref, lq1_ref, lk1_ref, lq2_ref, lk2_ref, g_ref,
                      qs_ref, kn_ref, vn_ref, *rest, pages_per_step, lambda_init, tk):
    pages = rest[:2 * pages_per_step]
    o_ref, os_ref = rest[2 * pages_per_step:2 * pages_per_step + 2]
    scratch = rest[2 * pages_per_step + 2:]
    lam_refs = (lq1_ref, lk1_ref, lq2_ref, lk2_ref, g_ref)
    _paged_attn_kernel(pt_ref, qs_ref, kn_ref, vn_ref, *lam_refs, *pages, os_ref, *scratch[N_ATTN_SCRATCH:],
                       pages_per_step=pages_per_step, lambda_init=lambda_init, step_axis=2)
    _attn_kernel(q_ref, k_ref, v_ref, *lam_refs, o_ref, *scratch[:N_ATTN_SCRATCH],
                 lambda_init=lambda_init, tk=tk)


def _attn_both(page_table, qkv, qkv3, cache_k, cache_v, prm, l, nbp, seq, tq, tk, lambda_init):
    m = qkv.shape[0]
    nq = seq // tq
    nbs, nt, _ = qkv3.shape
    n_pages = page_table.shape[0] // nbs
    page_rows = cache_k.shape[2]
    assert nbs == nbp * H_C and n_pages % nq == 0, (nbs, nbp, n_pages, nq)
    pages_per_step = n_pages // nq
    seq_of = lambda b, h: b * H_C + h
    lamspec = pl.BlockSpec((None, 1, HD_C), lambda b, h, qi, pt: (l, 0, 0))

    def page_spec(p):
        return pl.BlockSpec(
            (None, None, page_rows, LANES),
            lambda b, h, qi, pt: (l, pt[seq_of(b, h) * n_pages + qi * pages_per_step + p], 0, 0))

    def new_tokens_spec(part):
        return pl.BlockSpec((None, nt, W_C), lambda b, h, qi, pt: (seq_of(b, h), 0, part))

    grid_spec = pltpu.PrefetchScalarGridSpec(
        num_scalar_prefetch=1,
        grid=(nbp, H_C, nq),
        in_specs=[
            pl.BlockSpec((tq, DV_C), lambda b, h, qi, pt: (b * nq + qi, h)),
            pl.BlockSpec((seq, DV_C), lambda b, h, qi, pt: (b, H_C + h)),
            pl.BlockSpec((seq, DV_C), lambda b, h, qi, pt: (b, 2 * H_C + h)),
            lamspec, lamspec, lamspec, lamspec,
            pl.BlockSpec((None, 1, DV_C), lambda b, h, qi, pt: (l, 0, 0)),
            new_tokens_spec(0), new_tokens_spec(1), new_tokens_spec(2),
        ] + [page_spec(p) for p in range(pages_per_step)] * 2,
        out_specs=[
            pl.BlockSpec((tq, DV_C), lambda b, h, qi, pt: (b * nq + qi, h)),
            pl.BlockSpec((None, nt, W_C), lambda b, h, qi, pt: (seq_of(b, h), 0, 0)),
        ],
        scratch_shapes=[
            pltpu.VMEM((2, tq, LANES), F32),
            pltpu.VMEM((2, tq, LANES), F32),
            pltpu.VMEM((2, tq, DV_C), F32),
            pltpu.VMEM((2, 2, tq, tk), BF16),
            pltpu.VMEM((2, 2, tq, LANES), F32),
            pltpu.VMEM((nt * 2 * H_C, W_C), BF16),
            pltpu.VMEM((nt * 2 * H_C, 1), F32),
            pltpu.VMEM((nt * 2 * H_C, 1), F32),
            pltpu.VMEM((nt * 2 * H_C, W_C), F32),
        ],
    )
    return pl.pallas_call(
        functools.partial(_attn_both_kernel, pages_per_step=pages_per_step, lambda_init=lambda_init, tk=tk),
        grid_spec=grid_spec,
        out_shape=[jax.ShapeDtypeStruct((m, W_C), BF16), jax.ShapeDtypeStruct((nbs, nt, W_C), F32)],
        compiler_params=_cparams(("arbitrary",) * 3),
        name="attn_both",
    )(page_table, qkv, qkv, qkv, prm["lam_q1"], prm["lam_k1"], prm["lam_q2"], prm["lam_k2"],
      prm["subln_g"], qkv3, qkv3, qkv3, *([cache_k] * pages_per_step), *([cache_v] * pages_per_step))


def _values_to_head_rows(v):
    lead = v.shape[:-3]
    tokens = v.shape[-3]
    v = v.reshape(lead + (tokens, H_C, 2, LANES))
    v = jnp.swapaxes(v, -3, -2)
    return v.reshape(lead + (tokens * HEAD_ROWS, LANES))


def _head_rows_to_values(r, lead):
    depth = r.shape[0]
    v = r.reshape((depth,) + lead + (2, H_C, LANES))
    v = jnp.swapaxes(v, -3, -2)
    return v.reshape((depth,) + lead + (H_C, DV_C))


def _largest_tile(n, cap, mult):
    t = min(n, cap)
    while n % t or t % mult:
        t -= 1
    return t


def kernel(x_prompt, x_sample, cache_k, cache_v, state_lru_h, state_conv, page_table, w_in, w_out, conv_w,
           conv_b, w_a, b_a, w_x, b_x, lru_lam, vn_g, vn_b, w_s, b_s, lam_q1, lam_k1, lam_q2, lam_k2,
           subln_g, ln1_g, ln1_b, ln2_g, ln2_b, w_gate, w_up, w_down):
    depth = w_in.shape[0]
    nbp, seq, _ = x_prompt.shape
    nbs, nts, _ = x_sample.shape
    n_pool, page = cache_k.shape[1], cache_k.shape[2]
    alpha = (2.0 * depth) ** 0.25
    mp, ms = nbp * seq, nbs * nts

    row3 = lambda a: a.reshape(depth, 1, a.shape[-1])
    prm = {
        "w_a": w_a.astype(BF16), "w_x": w_x.astype(BF16),
        "conv_w": conv_w, "conv_b": row3(conv_b), "b_a": row3(b_a), "b_x": row3(b_x),
        "lru_lam": row3(lru_lam), "vn_g": row3(vn_g), "vn_b": row3(vn_b), "w_s": w_s,
        "bs_rows": jnp.repeat(jnp.transpose(b_s, (0, 2, 1)), GW_B, axis=-1),
        "w4": jnp.repeat(jnp.transpose(w_s[:, :, :nts, :nts], (0, 2, 3, 1)), GW_B, axis=-1),
        "b4": jnp.repeat(jnp.transpose(b_s[:, :, :nts], (0, 2, 1)), GW_B, axis=-1),
        "lam_q1": row3(lam_q1), "lam_k1": row3(lam_k1), "lam_q2": row3(lam_q2), "lam_k2": row3(lam_k2),
        "subln_g": row3(subln_g), "ln1_g": row3(ln1_g), "ln1_b": row3(ln1_b),
        "ln2_g": row3(ln2_g), "ln2_b": row3(ln2_b),
    }
    ck = cache_k.reshape(depth, n_pool, page * HEAD_ROWS, LANES)
    cv = _values_to_head_rows(cache_v)
    pt_flat = page_table.reshape(-1)
    sconv = jnp.transpose(state_conv, (0, 2, 1, 3))
    zero_cbuf = jnp.zeros((nbp, SUBLANES, W_A), F32)
    zero_h = jnp.zeros((nbp, 1, W_A), F32)

    tm_in = _largest_tile(mp, 1024, CHUNK)
    tm_p = _largest_tile(mp, 512, CHUNK)
    tt_a = _largest_tile(seq, 256, SUBLANES)
    tk = _largest_tile(seq, 512, CHUNK)
    tq = _largest_tile(tk, 512, CHUNK)
    tf = _largest_tile(D_FF, 512, 128)

    xp = x_prompt.reshape(mp, D_MODEL)
    xs = x_sample.reshape(ms, D_MODEL)
    xp_b, xs_b = xp.astype(BF16), xs.astype(BF16)
    kbuf_p = jnp.zeros((depth, mp * HEAD_ROWS, LANES), F32)
    vbuf_p = jnp.zeros((depth, mp * HEAD_ROWS, LANES), F32)
    kbuf_s = jnp.zeros((depth, ms * HEAD_ROWS, LANES), F32)
    vbuf_s = jnp.zeros((depth, ms * HEAD_ROWS, LANES), F32)
    outs = [[] for _ in range(5)]
    for l in range(depth):
        lambda_init = 0.8 - 0.6 * math.exp(-0.3 * l)
        za_s, qkv_s, kbuf_s, vbuf_s, w_in_b = _inproj(xs_b, w_in, l, kbuf_s, vbuf_s, l, ms, F32, True)
        ya_s, yb_s, h_s, c_s, vn_s = _sample_mix(za_s.reshape(-1, LANES), sconv, state_lru_h, prm, l, nbs,
                                                  nts)
        ya_s, yb_s = ya_s.reshape(ms, W_A), yb_s.reshape(ms, W_B)
        za, qkv, kbuf_p, vbuf_p = _inproj(xp_b, w_in_b, 0, kbuf_p, vbuf_p, l, tm_in, BF16, False)
        ya, h_p, c_p = _mix_a(za, zero_cbuf, zero_h, prm, l, nbp, tt_a)
        yb = _mix_b(za, prm, l, tm_p)
        o, o_s = _attn_both(pt_flat, qkv, qkv_s.reshape(nbs, nts, 3 * W_C), ck, cv, prm, l, nbp, seq, tq, tk,
                            lambda_init)
        xs, xs_b, w_out_b = _outproj(ya_s, yb_s, o_s.reshape(ms, W_C), xs, w_out, l, prm, l, ms, alpha, True)
        xs, xs_b, wg_b, wu_b, wd_b = _swiglu(xs, xs_b, w_gate, w_up, w_down, l, prm, l, ms, tf, alpha, True)
        xp, xp_b = _outproj(ya, yb, o, xp, w_out_b, 0, prm, l, tm_p, alpha, False)
        xp, xp_b = _swiglu(xp, xp_b, wg_b, wu_b, wd_b, 0, prm, l, tm_p, tf, alpha, False)

        for lst, val in zip(outs, (h_p, h_s, c_p, c_s, vn_s)):
            lst.append(val)

    hp, hs, cp, cs, chv = [jnp.stack(v) for v in outs]
    return (
        xp.reshape(nbp, seq, D_MODEL),
        xs.reshape(nbs, nts, D_MODEL),
        kbuf_p.reshape(depth, nbp, seq, H_C, 2, HD_C),
        _head_rows_to_values(vbuf_p, (nbp, seq)),
        kbuf_s.reshape(depth, nbs, nts, H_C, 2, HD_C),
        _head_rows_to_values(vbuf_s, (nbs, nts)),
        hp.reshape(depth, nbp, W_A),
        hs,
        cp[:, :, SUBLANES - (CONV_W - 1):, :],
        jnp.transpose(cs, (0, 2, 1, 3)),
        jnp.transpose(chv, (0, 2, 1, 3)),
    )
```
